```python
import jax, jax.numpy as jnp
from jax import lax
import numpy as np

D_MODEL = 1024
BATCH = 16
SEQ = 2048
DEPTH = 1
DEC_BATCH = 8
DEC_SEQ = 64
PAST_LEN = 4096

CHUNK = 64
QBLK = 128
SB_HEADS = 8
SB_HEAD_DIM = 128
SB_WIDTH = SB_HEADS * SB_HEAD_DIM
POOL_WINDOWS = (2, 4, 8, 16)
POOL_GROUPS = len(POOL_WINDOWS)
POOL_WIDTH = D_MODEL
POOL_GROUP_DIM = POOL_WIDTH // POOL_GROUPS
POOL_BUF = max(POOL_WINDOWS) - 1
IN_COLS = 4 * SB_WIDTH + 2 * POOL_WIDTH + 2 * D_MODEL
EPS = 1e-6

kernel_name = "stickbreak_pool_gated_hybrid_step"


def rmsnorm(x, g):
    xf = x.astype(jnp.float32)
    y = xf * lax.rsqrt(jnp.mean(xf * xf, axis=-1, keepdims=True) + EPS)
    return (y * g.astype(jnp.float32)).astype(x.dtype)


def split_proj(h, w_in):
    z = h @ w_in
    offs = np.cumsum([SB_WIDTH] * 4 + [POOL_WIDTH] * 2).tolist()
    q, k, v, ga, u, gb, mg = jnp.split(z, offs, axis=-1)
    b, t = h.shape[0], h.shape[1]
    hs = (b, t, SB_HEADS, SB_HEAD_DIM)
    return q.reshape(hs), k.reshape(hs), v.reshape(hs), ga, u, gb, mg


def stick_breaking(q, k, v, q_start):
    b, tq = q.shape[0], q.shape[1]
    tk = k.shape[1]
    z = jnp.einsum('bqhd,bkhd->bhqk', q, k,
                   preferred_element_type=jnp.float32) * (SB_HEAD_DIM ** -0.5)
    qpos = q_start + jnp.arange(tq)
    kpos = jnp.arange(tk)
    mask = kpos[None, :] < qpos[:, None]
    log_beta = jax.nn.log_sigmoid(z)
    log_1m = jnp.where(mask, log_beta - z, 0.0)
    after = lax.cumsum(log_1m, axis=3, reverse=True) - log_1m
    a = jnp.where(mask, jnp.exp(log_beta + after), 0.0)
    out = jnp.einsum('bhqk,bkhd->bqhd', a, v.astype(jnp.float32))
    return out.astype(q.dtype).reshape(b, tq, SB_WIDTH)


def multi_scale_pool(u, buf, pos_start, w_pool, pool_scale):
    b, t = u.shape[0], u.shape[1]
    ext = jnp.concatenate([buf.astype(u.dtype), u], axis=1).astype(jnp.float32)
    cs = jnp.cumsum(ext, axis=1)
    cs = jnp.concatenate([jnp.zeros_like(cs[:, :1]), cs], axis=1)
    pos = pos_start + jnp.arange(t)
    uf = u.astype(jnp.float32)
    outs = []
    for g, w in enumerate(POOL_WINDOWS):
        sl = slice(g * POOL_GROUP_DIM, (g + 1) * POOL_GROUP_DIM)
        hi = cs[:, POOL_BUF + 1:POOL_BUF + 1 + t, sl]
        lo = cs[:, POOL_BUF + 1 - w:POOL_BUF + 1 - w + t, sl]
        cnt = jnp.minimum(pos + 1, w).astype(jnp.float32)[None, :, None]
        outs.append((hi - lo) / cnt - uf[..., sl])
    p = jnp.concatenate(outs, axis=-1).astype(u.dtype)
    p = jnp.einsum('btgc,gcd->btgd', p.reshape(b, t, POOL_GROUPS, POOL_GROUP_DIM), w_pool)
    return p.reshape(b, t, POOL_WIDTH) * pool_scale


def merge_out(x, att, ga, pb, gb, mg, w_br_a, w_br_b, w_out):
    y_a = (att * jax.nn.silu(ga)) @ w_br_a
    y_b = (pb * jax.nn.silu(gb)) @ w_br_b
    gate = jax.nn.sigmoid(mg)
    m = gate[..., :D_MODEL] * y_a + gate[..., D_MODEL:] * y_b
    return x + m @ w_out


def setup_inputs(seed: int = 0) -> dict:
    key = jax.random.key(seed)
    ks = jax.random.split(key, 13)
    f = jnp.float32
    nrm = lambda k, s, sc: jax.random.normal(k, s, f) * sc
    return {
        "x_prompt": nrm(ks[0], (BATCH, SEQ, D_MODEL), 1.0),
        "x_sample": nrm(ks[1], (DEC_BATCH, DEC_SEQ, D_MODEL), 1.0),
        "cache_k": nrm(ks[2], (DEPTH, DEC_BATCH, PAST_LEN, SB_HEADS, SB_HEAD_DIM), 1.0),
        "cache_v": nrm(ks[3], (DEPTH, DEC_BATCH, PAST_LEN, SB_HEADS, SB_HEAD_DIM), 1.0),
        "state_pool": nrm(ks[4], (DEPTH, DEC_BATCH, POOL_BUF, POOL_WIDTH), 1.0),
        "norm_g": 1.0 + nrm(ks[5], (DEPTH, D_MODEL), 0.05),
        "w_in": nrm(ks[6], (DEPTH, D_MODEL, IN_COLS), D_MODEL ** -0.5),
        "w_pool": nrm(ks[7], (DEPTH, POOL_GROUPS, POOL_GROUP_DIM, POOL_GROUP_DIM), POOL_GROUP_DIM ** -0.5),
        "pool_scale": 1.0 + nrm(ks[8], (DEPTH, POOL_WIDTH), 0.1),
        "w_br_a": nrm(ks[9], (DEPTH, SB_WIDTH, D_MODEL), SB_WIDTH ** -0.5),
        "w_br_b": nrm(ks[10], (DEPTH, POOL_WIDTH, D_MODEL), POOL_WIDTH ** -0.5),
        "w_out": nrm(ks[11], (DEPTH, D_MODEL, D_MODEL), D_MODEL ** -0.5),
        "final_g": 1.0 + nrm(ks[12], (D_MODEL,), 0.05),
    }


def reference(x_prompt, x_sample, cache_k, cache_v, state_pool, norm_g, w_in, w_pool,
              pool_scale, w_br_a, w_br_b, w_out, final_g):
    xp, xs = x_prompt, x_sample
    past_len = cache_k.shape[2]
    kp_l, vp_l, pp_l, ks_l, vs_l, ps_l = [], [], [], [], [], []
    for l in range(DEPTH):
        h = rmsnorm(xp, norm_g[l])
        q, k, v, ga, u, gb, mg = split_proj(h, w_in[l])
        n_blk = xp.shape[1] // QBLK
        att = jnp.concatenate(
            [stick_breaking(q[:, i * QBLK:(i + 1) * QBLK], k[:, :(i + 1) * QBLK],
                            v[:, :(i + 1) * QBLK], i * QBLK) for i in range(n_blk)], axis=1)
        zero_buf = jnp.zeros((xp.shape[0], POOL_BUF, POOL_WIDTH), u.dtype)
        pb = multi_scale_pool(u, zero_buf, 0, w_pool[l], pool_scale[l])
        xp = merge_out(xp, att, ga, pb, gb, mg, w_br_a[l], w_br_b[l], w_out[l])
        kp_l.append(k)
        vp_l.append(v)
        pp_l.append(u[:, -POOL_BUF:])

        h = rmsnorm(xs, norm_g[l])
        q, k, v, ga, u, gb, mg = split_proj(h, w_in[l])
        k_all = jnp.concatenate([cache_k[l].astype(k.dtype), k], axis=1)
        v_all = jnp.concatenate([cache_v[l].astype(v.dtype), v], axis=1)
        att = stick_breaking(q, k_all, v_all, past_len)
        buf = state_pool[l].astype(u.dtype)
        pb = multi_scale_pool(u, buf, past_len, w_pool[l], pool_scale[l])
        xs = merge_out(xs, att, ga, pb, gb, mg, w_br_a[l], w_br_b[l], w_out[l])
        ks_l.append(k)
        vs_l.append(v)
        ps_l.append(jnp.concatenate([buf, u], axis=1)[:, -POOL_BUF:])

    y_prompt = rmsnorm(xp, final_g)
    y_sample = rmsnorm(xs, final_g)
    return (y_prompt, y_sample, jnp.stack(kp_l), jnp.stack(vp_l), jnp.stack(pp_l),
            jnp.stack(ks_l), jnp.stack(vs_l), jnp.stack(ps_l))
```

```python
import functools

import jax
import jax.numpy as jnp
from jax import lax
from jax.experimental import pallas as pl
from jax.experimental.pallas import tpu as pltpu

F32 = jnp.float32
BF16 = jnp.bfloat16

HEADS = 8
HEAD_DIM = 128
POOL_WINDOWS = (2, 4, 8, 16)
POOL_HIST = 16
EPS = 1e-6
SCALE = HEAD_DIM ** -0.5

TQ = 256
TKC = 512
TKS = 256
VMEM_LIMIT = 56 * 1024 * 1024


def _rms(x, g):
    return x * lax.rsqrt(jnp.mean(x * x, axis=-1, keepdims=True) + EPS) * g


def _silu(x):
    return x * jax.nn.sigmoid(x)


def _tri(n):
    j = lax.broadcasted_iota(jnp.int32, (n, n), 0)
    s = lax.broadcasted_iota(jnp.int32, (n, n), 1)
    return (j > s).astype(BF16)


def _causal_mask(n):
    r = lax.broadcasted_iota(jnp.int32, (n, n), 0)
    c = lax.broadcasted_iota(jnp.int32, (n, n), 1)
    return c < r


def _sb_block(q, k, v, tri, acc_ref, carry_ref, mask):
    s = lax.dot_general(q, k, (((1,), (1,)), ((), ())), preferred_element_type=F32) * SCALE
    l1p = jnp.log(1.0 + jnp.exp(-jnp.abs(s)))
    log_1m = -jnp.maximum(s, 0.0) - l1p
    log_beta = jnp.minimum(s, 0.0) - l1p
    if mask is not None:
        log_1m = jnp.where(mask, log_1m, 0.0)
    carry = carry_ref[...]
    after = jnp.dot(log_1m.astype(BF16), tri, preferred_element_type=F32) + carry
    a = jnp.exp(log_beta + after)
    if mask is not None:
        a = jnp.where(mask, a, 0.0)
    acc_ref[...] += jnp.dot(a.astype(BF16), v, preferred_element_type=F32)
    carry_ref[...] = carry + jnp.sum(log_1m, axis=-1, keepdims=True)


def _pool(uext_ref, rows, pos0, wpool_ref, pscale_ref):
    n = POOL_HIST + rows
    gdim = uext_ref.shape[1] // len(POOL_WINDOWS)
    pos = pos0 + lax.broadcasted_iota(jnp.int32, (rows, 1), 0)
    outs = []
    for g, w in enumerate(POOL_WINDOWS):
        cols = slice(g * gdim, (g + 1) * gdim)
        e = uext_ref[:, cols]
        acc = e
        span = 1
        while span < w:
            acc = acc + pltpu.roll(acc, span, axis=0)
            span *= 2
        inv = 1.0 / jnp.minimum(pos + 1, w).astype(F32)
        p = acc[POOL_HIST:n] * inv - e[POOL_HIST:n]
        o = jnp.dot(p.astype(BF16), wpool_ref[g], preferred_element_type=F32)
        outs.append(o)
    return jnp.concatenate(outs, axis=-1) * pscale_ref[...]


def _merge(x, att_g, pb_g, gate_a, gate_b, wa_ref, wb_ref, wout_ref, fg):
    y_a = jnp.dot(att_g, wa_ref[...], preferred_element_type=F32)
    y_b = jnp.dot(pb_g, wb_ref[...], preferred_element_type=F32)
    m = gate_a * y_a + gate_b * y_b
    out = x + jnp.dot(m.astype(BF16), wout_ref[...], preferred_element_type=F32)
    return _rms(out, fg)


def _prompt_kernel(x_ref, ng_ref, win_ref, wpool_ref, pscale_ref, wa_ref, wb_ref, wout_ref, fg_ref,
                   y_ref, k_ref, v_ref, pool_ref,
                   kh_ref, vh_ref, q_ref, sga_ref, uext_ref, sgb_ref, gate_ref, attg_ref,
                   acc_ref, carry_ref, tri_ref):
    t = pl.program_id(1)
    nt = pl.num_programs(1)
    d = x_ref.shape[2]
    row0 = pl.multiple_of(t * TQ, TQ)

    @pl.when(t == 0)
    def _():
        tri_ref[...] = _tri(TQ)
        uext_ref[0:POOL_HIST, :] = jnp.zeros((POOL_HIST, d), F32)

    @pl.when(t > 0)
    def _():
        uext_ref[0:POOL_HIST, :] = uext_ref[TQ:TQ + POOL_HIST, :]

    x = x_ref[0]
    h = _rms(x, ng_ref[...]).astype(BF16)

    def proj(c):
        return jnp.dot(h, win_ref[:, c * d:(c + 1) * d], preferred_element_type=F32)

    q = proj(0).astype(BF16)
    for hd in range(HEADS):
        q_ref[hd] = q[:, hd * HEAD_DIM:(hd + 1) * HEAD_DIM]
    k = proj(1)
    k_ref[0] = k
    kb = k.astype(BF16)
    for hd in range(HEADS):
        kh_ref[hd, pl.ds(row0, TQ), :] = kb[:, hd * HEAD_DIM:(hd + 1) * HEAD_DIM]
    v = proj(2)
    v_ref[0] = v
    vb = v.astype(BF16)
    for hd in range(HEADS):
        vh_ref[hd, pl.ds(row0, TQ), :] = vb[:, hd * HEAD_DIM:(hd + 1) * HEAD_DIM]
    sga_ref[...] = _silu(proj(3)).astype(BF16)
    uext_ref[POOL_HIST:POOL_HIST + TQ, :] = proj(4)
    sgb_ref[...] = _silu(proj(5)).astype(BF16)
    gate_ref[:, 0:d] = jax.nn.sigmoid(proj(6))
    gate_ref[:, d:2 * d] = jax.nn.sigmoid(proj(7))

    @pl.when(t == nt - 1)
    def _():
        pool_ref[0] = uext_ref[TQ:TQ + POOL_HIST, :]

    mask = _causal_mask(TQ)
    for hd in range(HEADS):
        acc_ref[...] = jnp.zeros_like(acc_ref)
        carry_ref[...] = jnp.zeros_like(carry_ref)
        qh = q_ref[hd]
        _sb_block(qh, kh_ref[hd, pl.ds(row0, TQ), :], vh_ref[hd, pl.ds(row0, TQ), :],
                  tri_ref[...], acc_ref, carry_ref, mask)

        def body(i, _, hd=hd, qh=qh):
            r = pl.multiple_of((t - 1 - i) * TQ, TQ)
            _sb_block(qh, kh_ref[hd, pl.ds(r, TQ), :], vh_ref[hd, pl.ds(r, TQ), :],
                      tri_ref[...], acc_ref, carry_ref, None)
            return 0

        lax.fori_loop(0, t, body, 0)
        cols = slice(hd * HEAD_DIM, (hd + 1) * HEAD_DIM)
        attg_ref[:, cols] = (acc_ref[...] * sga_ref[:, cols].astype(F32)).astype(BF16)

    pb = _pool(uext_ref, TQ, t * TQ, wpool_ref, pscale_ref)
    pb_g = (pb * sgb_ref[...].astype(F32)).astype(BF16)

    y_ref[0] = _merge(x_ref[0], attg_ref[...], pb_g, gate_ref[:, 0:d], gate_ref[:, d:2 * d],
                      wa_ref, wb_ref, wout_ref, fg_ref[...])


def _const_spec(shape):
    nd = len(shape)
    return pl.BlockSpec(shape, lambda *_: (0,) * nd, pipeline_mode=pl.Buffered(1))


def _prompt_layer(x, ng, win, wpool, pscale, wa, wb, wout, fg):
    b, t, d = x.shape
    nt = t // TQ
    tile = pl.BlockSpec((1, TQ, d), lambda i, j: (i, j, 0))
    out_shape = (
        jax.ShapeDtypeStruct((b, t, d), F32),
        jax.ShapeDtypeStruct((b, t, d), F32),
        jax.ShapeDtypeStruct((b, t, d), F32),
        jax.ShapeDtypeStruct((b, POOL_HIST, d), F32),
    )
    return pl.pallas_call(
        _prompt_kernel,
        grid=(b, nt),
        in_specs=[tile, _const_spec(ng.shape), _const_spec(win.shape), _const_spec(wpool.shape),
                  _const_spec(pscale.shape), _const_spec(wa.shape), _const_spec(wb.shape),
                  _const_spec(wout.shape), _const_spec(fg.shape)],
        out_specs=(tile, tile, tile, pl.BlockSpec((1, POOL_HIST, d), lambda i, j: (i, 0, 0))),
        out_shape=out_shape,
        scratch_shapes=[
            pltpu.VMEM((HEADS, t, HEAD_DIM), BF16),
            pltpu.VMEM((HEADS, t, HEAD_DIM), BF16),
            pltpu.VMEM((HEADS, TQ, HEAD_DIM), BF16),
            pltpu.VMEM((TQ, d), BF16),
            pltpu.VMEM((POOL_HIST + TQ, d), F32),
            pltpu.VMEM((TQ, d), BF16),
            pltpu.VMEM((TQ, 2 * d), F32),
            pltpu.VMEM((TQ, d), BF16),
            pltpu.VMEM((TQ, HEAD_DIM), F32),
            pltpu.VMEM((TQ, 1), F32),
            pltpu.VMEM((TQ, TQ), BF16),
        ],
        compiler_params=pltpu.CompilerParams(
            dimension_semantics=("arbitrary", "arbitrary"), vmem_limit_bytes=VMEM_LIMIT),
        name="prompt_layer",
    )(x, ng, win, wpool, pscale, wa, wb, wout, fg)


def _sample_proj_kernel(x_ref, ng_ref, win_ref, z_ref):
    h = _rms(x_ref[...], ng_ref[...]).astype(BF16)
    z_ref[0] = jnp.dot(h, win_ref[...], preferred_element_type=F32)


def _sample_proj(x2, ng, win):
    n, d = x2.shape
    nc = win.shape[1] // d
    return pl.pallas_call(
        _sample_proj_kernel,
        grid=(nc,),
        in_specs=[pl.BlockSpec((n, d), lambda c: (0, 0)), pl.BlockSpec(ng.shape, lambda c: (0, 0)),
                  pl.BlockSpec((d, d), lambda c: (0, c))],
        out_specs=pl.BlockSpec((1, n, d), lambda c: (c, 0, 0)),
        out_shape=jax.ShapeDtypeStruct((nc, n, d), F32),
        compiler_params=pltpu.CompilerParams(dimension_semantics=("arbitrary",)),
        name="sample_proj",
    )(x2, ng, win)


def _sample_kernel(x_ref, z_ref, ck_ref, cv_ref, hist_ref, wpool_ref, pscale_ref, wa_ref, wb_ref,
                   wout_ref, fg_ref, y_ref, pool_ref, acc_ref, carry_ref, uext_ref, tri_ref,
                   *, past_len):
    s = pl.program_id(1)
    ns = pl.num_programs(1)
    ts = x_ref.shape[1]
    d = x_ref.shape[2]

    @pl.when(s == 0)
    def _():
        tri_ref[...] = _tri(TKS)
        tri_s = _tri(ts)
        mask = _causal_mask(ts)
        for hd in range(HEADS):
            cols = slice(hd * HEAD_DIM, (hd + 1) * HEAD_DIM)
            acc_ref[hd] = jnp.zeros((ts, HEAD_DIM), F32)
            carry_ref[hd] = jnp.zeros((ts, 1), F32)
            _sb_block(z_ref[0, :, cols].astype(BF16), z_ref[1, :, cols].astype(BF16),
                      z_ref[2, :, cols].astype(BF16), tri_s, acc_ref.at[hd], carry_ref.at[hd], mask)

    @pl.when(s > 0)
    def _():
        for hd in range(HEADS):
            cols = slice(hd * HEAD_DIM, (hd + 1) * HEAD_DIM)
            qh = z_ref[0, :, cols].astype(BF16)
            for sub in reversed(range(TKC // TKS)):
                rows = slice(sub * TKS, (sub + 1) * TKS)
                _sb_block(qh, ck_ref[0, rows, cols].astype(BF16), cv_ref[0, rows, cols].astype(BF16),
                          tri_ref[...], acc_ref.at[hd], carry_ref.at[hd], None)

    @pl.when(s == ns - 1)
    def _():
        att = jnp.concatenate([acc_ref[hd] for hd in range(HEADS)], axis=-1)
        att_g = (att * _silu(z_ref[3])).astype(BF16)
        uext_ref[0:POOL_HIST, :] = hist_ref[0]
        uext_ref[POOL_HIST:POOL_HIST + ts, :] = z_ref[4]
        pool_ref[0] = uext_ref[ts:ts + POOL_HIST, :]
        pb = _pool(uext_ref, ts, past_len, wpool_ref, pscale_ref)
        pb_g = (pb * _silu(z_ref[5])).astype(BF16)
        y_ref[0] = _merge(x_ref[0], att_g, pb_g, jax.nn.sigmoid(z_ref[6]), jax.nn.sigmoid(z_ref[7]),
                          wa_ref, wb_ref, wout_ref, fg_ref[...])


def _sample_layer(x, z, ck, cv, hist, wpool, pscale, wa, wb, wout, fg):
    b, ts, d = x.shape
    past_len = ck.shape[1]
    nkb = past_len // TKC
    nc = z.shape[0]

    def cache_map(i, s):
        return (i, nkb - jnp.maximum(s, 1), 0)

    return pl.pallas_call(
        functools.partial(_sample_kernel, past_len=past_len),
        grid=(b, nkb + 1),
        in_specs=[pl.BlockSpec((1, ts, d), lambda i, s: (i, 0, 0)),
                  pl.BlockSpec((nc, ts, d), lambda i, s: (0, i, 0)),
                  pl.BlockSpec((1, TKC, d), cache_map),
                  pl.BlockSpec((1, TKC, d), cache_map),
                  pl.BlockSpec((1, POOL_HIST, d), lambda i, s: (i, 0, 0)),
                  _const_spec(wpool.shape), _const_spec(pscale.shape), _const_spec(wa.shape),
                  _const_spec(wb.shape), _const_spec(wout.shape), _const_spec(fg.shape)],
        out_specs=(pl.BlockSpec((1, ts, d), lambda i, s: (i, 0, 0)),
                   pl.BlockSpec((1, POOL_HIST, d), lambda i, s: (i, 0, 0))),
        out_shape=(jax.ShapeDtypeStruct((b, ts, d), F32),
                   jax.ShapeDtypeStruct((b, POOL_HIST, d), F32)),
        scratch_shapes=[
            pltpu.VMEM((HEADS, ts, HEAD_DIM), F32),
            pltpu.VMEM((HEADS, ts, 1), F32),
            pltpu.VMEM((POOL_HIST + ts, d), F32),
            pltpu.VMEM((TKS, TKS), BF16),
        ],
        compiler_params=pltpu.CompilerParams(
            dimension_semantics=("arbitrary", "arbitrary"), vmem_limit_bytes=VMEM_LIMIT),
        name="sample_layer",
    )(x, z, ck, cv, hist, wpool, pscale, wa, wb, wout, fg)


def kernel(x_prompt, x_sample, cache_k, cache_v, state_pool, norm_g, w_in, w_pool, pool_scale,
           w_br_a, w_br_b, w_out, final_g):
    depth = norm_g.shape[0]
    assert depth == 1, "single-layer stack"
    b, t, d = x_prompt.shape
    bs, ts, _ = x_sample.shape
    past_len = cache_k.shape[2]
    assert t % TQ == 0 and past_len % TKC == 0 and d == HEADS * HEAD_DIM

    ng = norm_g[0].reshape(1, d)
    fg = final_g.reshape(1, d)
    pscale = pool_scale[0].reshape(1, d)
    win = w_in[0].astype(BF16)
    wpool = w_pool[0].astype(BF16)
    wa = w_br_a[0].astype(BF16)
    wb = w_br_b[0].astype(BF16)
    wout = w_out[0].astype(BF16)

    y_p, k_p, v_p, pool_p = _prompt_layer(x_prompt, ng, win, wpool, pscale, wa, wb, wout, fg)

    z = _sample_proj(x_sample.reshape(bs * ts, d), ng, win)
    hist = jnp.pad(state_pool[0], ((0, 0), (POOL_HIST - state_pool.shape[2], 0), (0, 0)))
    y_s, pool_s = _sample_layer(x_sample, z, cache_k[0].reshape(bs, past_len, d),
                                cache_v[0].reshape(bs, past_len, d), hist, wpool, pscale,
                                wa, wb, wout, fg)

    nbuf = state_pool.shape[2]
    hs = (HEADS, HEAD_DIM)
    return (y_p, y_s,
            k_p.reshape(1, b, t, *hs), v_p.reshape(1, b, t, *hs), pool_p[None, :, POOL_HIST - nbuf:],
            z[1].reshape(1, bs, ts, *hs), z[2].reshape(1, bs, ts, *hs), pool_s[None, :, POOL_HIST - nbuf:])
```

```python
import functools

import jax
import jax.numpy as jnp
from jax import lax
from jax.experimental import pallas as pl
from jax.experimental.pallas import tpu as pltpu

F32 = jnp.float32
BF16 = jnp.bfloat16

HEADS = 8
HEAD_DIM = 128
POOL_WINDOWS = (2, 4, 8, 16)
POOL_HIST = 16
EPS = 1e-6
SCALE = HEAD_DIM ** -0.5
LOG2E = 1.4426950408889634

TQ = 256
TKC = 512
TKS = 256
VMEM_LIMIT = 56 * 1024 * 1024


def _rms(x, g):
    return x * lax.rsqrt(jnp.mean(x * x, axis=-1, keepdims=True) + EPS) * g


def _silu(x):
    return x * jax.nn.sigmoid(x)


def _tri(n):
    j = lax.broadcasted_iota(jnp.int32, (n, n), 0)
    s = lax.broadcasted_iota(jnp.int32, (n, n), 1)
    return (j > s).astype(BF16)


def _causal_mask(n):
    r = lax.broadcasted_iota(jnp.int32, (n, n), 0)
    c = lax.broadcasted_iota(jnp.int32, (n, n), 1)
    return c < r


def _head_rows(ref, lead, row0, rows, hd):
    return ref[lead, pl.ds(row0 * HEADS + hd, rows, stride=HEADS), :]


def _sb_block(q, k, v, tri, acc_ref, used_ref, mask, first):
    tk = k.shape[0]
    s = lax.dot_general(q, k, (((1,), (1,)), ((), ())), preferred_element_type=F32) * SCALE
    l1p = jnp.log(1.0 + jnp.exp2(jnp.abs(s) * (-LOG2E)))
    neg_log_1m = jnp.maximum(s, 0.0) + l1p
    log_beta = s - neg_log_1m
    if mask is not None:
        neg_log_1m = jnp.where(mask, neg_log_1m, 0.0)
    later = jnp.dot(neg_log_1m.astype(BF16), tri, preferred_element_type=F32)
    if not first:
        used = used_ref[...]
        later = later + jnp.concatenate([used] * (tk // HEAD_DIM), axis=-1)
    a = jnp.exp(log_beta - later)
    if mask is not None:
        a = jnp.where(mask, a, 0.0)
    pv = jnp.dot(a.astype(BF16), v, preferred_element_type=F32)
    tot = jnp.broadcast_to(jnp.sum(neg_log_1m, axis=-1, keepdims=True), used_ref.shape)
    if first:
        acc_ref[...] = pv
        used_ref[...] = tot
    else:
        acc_ref[...] += pv
        used_ref[...] = used + tot


def _pool(uext_ref, rows, pos0, wpool_ref, pscale_ref):
    n = POOL_HIST + rows
    gdim = uext_ref.shape[1] // len(POOL_WINDOWS)
    pos = pos0 + lax.broadcasted_iota(jnp.int32, (rows, 1), 0)
    outs = []
    for g, w in enumerate(POOL_WINDOWS):
        cols = slice(g * gdim, (g + 1) * gdim)
        e = uext_ref[:, cols]
        acc = e
        span = 1
        while span < w:
            acc = acc + pltpu.roll(acc, span, axis=0)
            span *= 2
        inv = 1.0 / jnp.minimum(pos + 1, w).astype(F32)
        p = acc[POOL_HIST:n] * inv - e[POOL_HIST:n]
        o = jnp.dot(p.astype(BF16), wpool_ref[g], preferred_element_type=F32)
        outs.append(o)
    return jnp.concatenate(outs, axis=-1) * pscale_ref[...]


def _merge(x, att_g, pb_g, gate_a, gate_b, wa_ref, wb_ref, wout_ref, fg):
    y_a = jnp.dot(att_g, wa_ref[...], preferred_element_type=F32)
    y_b = jnp.dot(pb_g, wb_ref[...], preferred_element_type=F32)
    m = gate_a * y_a + gate_b * y_b
    out = x + jnp.dot(m.astype(BF16), wout_ref[...], preferred_element_type=F32)
    return _rms(out, fg)


def _prompt_kernel(x_ref, ng_ref, win_ref, wpool_ref, pscale_ref, wa_ref, wb_ref, wout_ref, fg_ref,
                   y_ref, k_ref, v_ref, pool_ref,
                   kh_ref, vh_ref, q_ref, sga_ref, uext_ref, sgb_ref, gate_ref, attg_ref,
                   acc_ref, used_ref, tri_ref):
    t = pl.program_id(1)
    nt = pl.num_programs(1)
    d = x_ref.shape[2]
    row0 = pl.multiple_of(t * TQ, TQ)

    @pl.when(t == 0)
    def _():
        tri_ref[...] = _tri(TQ)
        uext_ref[0:POOL_HIST, :] = jnp.zeros((POOL_HIST, d), F32)

    @pl.when(t > 0)
    def _():
        uext_ref[0:POOL_HIST, :] = uext_ref[TQ:TQ + POOL_HIST, :]

    x = x_ref[0]
    h = _rms(x, ng_ref[...]).astype(BF16)

    def proj(c):
        return jnp.dot(h, win_ref[:, c * d:(c + 1) * d], preferred_element_type=F32)

    def head(z, hd):
        return z[:, hd * HEAD_DIM:(hd + 1) * HEAD_DIM]

    q = proj(0).astype(BF16)
    for hd in range(HEADS):
        q_ref[hd] = head(q, hd)
    k = proj(1)
    kb = k.astype(BF16)
    for hd in range(HEADS):
        k_ref[0, pl.ds(hd, TQ, stride=HEADS), :] = head(k, hd)
        kh_ref[hd, pl.ds(row0, TQ), :] = head(kb, hd)
    v = proj(2)
    vb = v.astype(BF16)
    for hd in range(HEADS):
        v_ref[0, pl.ds(hd, TQ, stride=HEADS), :] = head(v, hd)
        vh_ref[hd, pl.ds(row0, TQ), :] = head(vb, hd)
    sga_ref[...] = _silu(proj(3)).astype(BF16)
    uext_ref[POOL_HIST:POOL_HIST + TQ, :] = proj(4)
    sgb_ref[...] = _silu(proj(5)).astype(BF16)
    gate_ref[:, 0:d] = jax.nn.sigmoid(proj(6))
    gate_ref[:, d:2 * d] = jax.nn.sigmoid(proj(7))

    @pl.when(t == nt - 1)
    def _():
        pool_ref[0] = uext_ref[TQ:TQ + POOL_HIST, :]

    def attend(r, mask, first):
        for hd in range(HEADS):
            _sb_block(q_ref[hd], kh_ref[hd, pl.ds(r, TQ), :], vh_ref[hd, pl.ds(r, TQ), :],
                      tri_ref[...], acc_ref.at[hd], used_ref.at[hd], mask, first)

    attend(row0, _causal_mask(TQ), True)

    def body(i, carry):
        attend(pl.multiple_of((t - 1 - i) * TQ, TQ), None, False)
        return carry

    lax.fori_loop(0, t, body, 0)
    for hd in range(HEADS):
        cols = slice(hd * HEAD_DIM, (hd + 1) * HEAD_DIM)
        attg_ref[:, cols] = (acc_ref[hd] * sga_ref[:, cols].astype(F32)).astype(BF16)

    pb = _pool(uext_ref, TQ, t * TQ, wpool_ref, pscale_ref)
    pb_g = (pb * sgb_ref[...].astype(F32)).astype(BF16)

    y_ref[0] = _merge(x_ref[0], attg_ref[...], pb_g, gate_ref[:, 0:d], gate_ref[:, d:2 * d],
                      wa_ref, wb_ref, wout_ref, fg_ref[...])


def _const_spec(shape):
    nd = len(shape)
    return pl.BlockSpec(shape, lambda *_: (0,) * nd, pipeline_mode=pl.Buffered(1))


def _prompt_layer(x, ng, win, wpool, pscale, wa, wb, wout, fg):
    b, t, d = x.shape
    nt = t // TQ
    tile = pl.BlockSpec((1, TQ, d), lambda i, j: (i, j, 0))
    kv_tile = pl.BlockSpec((1, TQ * HEADS, HEAD_DIM), lambda i, j: (i, j, 0))
    kv_shape = jax.ShapeDtypeStruct((b, t * HEADS, HEAD_DIM), F32)
    out_shape = (
        jax.ShapeDtypeStruct((b, t, d), F32),
        kv_shape,
        kv_shape,
        jax.ShapeDtypeStruct((b, POOL_HIST, d), F32),
    )
    return pl.pallas_call(
        _prompt_kernel,
        grid=(b, nt),
        in_specs=[tile, _const_spec(ng.shape), _const_spec(win.shape), _const_spec(wpool.shape),
                  _const_spec(pscale.shape), _const_spec(wa.shape), _const_spec(wb.shape),
                  _const_spec(wout.shape), _const_spec(fg.shape)],
        out_specs=(tile, kv_tile, kv_tile, pl.BlockSpec((1, POOL_HIST, d), lambda i, j: (i, 0, 0))),
        out_shape=out_shape,
        scratch_shapes=[
            pltpu.VMEM((HEADS, t, HEAD_DIM), BF16),
            pltpu.VMEM((HEADS, t, HEAD_DIM), BF16),
            pltpu.VMEM((HEADS, TQ, HEAD_DIM), BF16),
            pltpu.VMEM((TQ, d), BF16),
            pltpu.VMEM((POOL_HIST + TQ, d), F32),
            pltpu.VMEM((TQ, d), BF16),
            pltpu.VMEM((TQ, 2 * d), F32),
            pltpu.VMEM((TQ, d), BF16),
            pltpu.VMEM((HEADS, TQ, HEAD_DIM), F32),
            pltpu.VMEM((HEADS, TQ, HEAD_DIM), F32),
            pltpu.VMEM((TQ, TQ), BF16),
        ],
        compiler_params=pltpu.CompilerParams(
            dimension_semantics=("arbitrary", "arbitrary"), vmem_limit_bytes=VMEM_LIMIT),
        name="prompt_layer",
    )(x, ng, win, wpool, pscale, wa, wb, wout, fg)


def _sample_proj_kernel(x_ref, ng_ref, win_ref, z_ref):
    h = _rms(x_ref[...], ng_ref[...]).astype(BF16)
    z_ref[0] = jnp.dot(h, win_ref[...], preferred_element_type=F32)


def _sample_proj(x2, ng, win):
    n, d = x2.shape
    nc = win.shape[1] // d
    return pl.pallas_call(
        _sample_proj_kernel,
        grid=(nc,),
        in_specs=[pl.BlockSpec((n, d), lambda c: (0, 0)), pl.BlockSpec(ng.shape, lambda c: (0, 0)),
                  pl.BlockSpec((d, d), lambda c: (0, c))],
        out_specs=pl.BlockSpec((1, n, d), lambda c: (c, 0, 0)),
        out_shape=jax.ShapeDtypeStruct((nc, n, d), F32),
        compiler_params=pltpu.CompilerParams(dimension_semantics=("arbitrary",)),
        name="sample_proj",
    )(x2, ng, win)


def _sample_kernel(x_ref, z_ref, ck_ref, cv_ref, hist_ref, wpool_ref, pscale_ref, wa_ref, wb_ref,
                   wout_ref, fg_ref, y_ref, pool_ref, acc_ref, used_ref, uext_ref, tri_ref,
                   *, past_len):
    s = pl.program_id(1)
    ns = pl.num_programs(1)
    ts = x_ref.shape[1]
    d = x_ref.shape[2]

    @pl.when(s == 0)
    def _():
        tri_ref[...] = _tri(TKS)
        tri_s = _tri(ts)
        mask = _causal_mask(ts)
        for hd in range(HEADS):
            cols = slice(hd * HEAD_DIM, (hd + 1) * HEAD_DIM)
            _sb_block(z_ref[0, :, cols].astype(BF16), z_ref[1, :, cols].astype(BF16),
                      z_ref[2, :, cols].astype(BF16), tri_s, acc_ref.at[hd], used_ref.at[hd],
                      mask, True)

    @pl.when(s > 0)
    def _():
        for hd in range(HEADS):
            cols = slice(hd * HEAD_DIM, (hd + 1) * HEAD_DIM)
            qh = z_ref[0, :, cols].astype(BF16)
            for sub in reversed(range(TKC // TKS)):
                _sb_block(qh, _head_rows(ck_ref, 0, sub * TKS, TKS, hd).astype(BF16),
                          _head_rows(cv_ref, 0, sub * TKS, TKS, hd).astype(BF16),
                          tri_ref[...], acc_ref.at[hd], used_ref.at[hd], None, False)

    @pl.when(s == ns - 1)
    def _():
        att = jnp.concatenate([acc_ref[hd] for hd in range(HEADS)], axis=-1)
        att_g = (att * _silu(z_ref[3])).astype(BF16)
        uext_ref[0:POOL_HIST, :] = hist_ref[0]
        uext_ref[POOL_HIST:POOL_HIST + ts, :] = z_ref[4]
        pool_ref[0] = uext_ref[ts:ts + POOL_HIST, :]
        pb = _pool(uext_ref, ts, past_len, wpool_ref, pscale_ref)
        pb_g = (pb * _silu(z_ref[5])).astype(BF16)
        y_ref[0] = _merge(x_ref[0], att_g, pb_g, jax.nn.sigmoid(z_ref[6]), jax.nn.sigmoid(z_ref[7]),
                          wa_ref, wb_ref, wout_ref, fg_ref[...])


def _sample_layer(x, z, ck, cv, hist, wpool, pscale, wa, wb, wout, fg):
    b, ts, d = x.shape
    past_len = ck.shape[1] // HEADS
    nkb = past_len // TKC
    nc = z.shape[0]

    def cache_map(i, s):
        return (i, nkb - jnp.maximum(s, 1), 0)

    cache_spec = pl.BlockSpec((1, TKC * HEADS, HEAD_DIM), cache_map)
    return pl.pallas_call(
        functools.partial(_sample_kernel, past_len=past_len),
        grid=(b, nkb + 1),
        in_specs=[pl.BlockSpec((1, ts, d), lambda i, s: (i, 0, 0)),
                  pl.BlockSpec((nc, ts, d), lambda i, s: (0, i, 0)),
                  cache_spec, cache_spec,
                  pl.BlockSpec((1, POOL_HIST, d), lambda i, s: (i, 0, 0)),
                  _const_spec(wpool.shape), _const_spec(pscale.shape), _const_spec(wa.shape),
                  _const_spec(wb.shape), _const_spec(wout.shape), _const_spec(fg.shape)],
        out_specs=(pl.BlockSpec((1, ts, d), lambda i, s: (i, 0, 0)),
                   pl.BlockSpec((1, POOL_HIST, d), lambda i, s: (i, 0, 0))),
        out_shape=(jax.ShapeDtypeStruct((b, ts, d), F32),
                   jax.ShapeDtypeStruct((b, POOL_HIST, d), F32)),
        scratch_shapes=[
            pltpu.VMEM((HEADS, ts, HEAD_DIM), F32),
            pltpu.VMEM((HEADS, ts, HEAD_DIM), F32),
            pltpu.VMEM((POOL_HIST + ts, d), F32),
            pltpu.VMEM((TKS, TKS), BF16),
        ],
        compiler_params=pltpu.CompilerParams(
            dimension_semantics=("arbitrary", "arbitrary"), vmem_limit_bytes=VMEM_LIMIT),
        name="sample_layer",
    )(x, z, ck, cv, hist, wpool, pscale, wa, wb, wout, fg)


def kernel(x_prompt, x_sample, cache_k, cache_v, state_pool, norm_g, w_in, w_pool, pool_scale,
           w_br_a, w_br_b, w_out, final_g):
    depth = norm_g.shape[0]
    assert depth == 1, "single-layer stack"
    b, t, d = x_prompt.shape
    bs, ts, _ = x_sample.shape
    past_len = cache_k.shape[2]
    assert t % TQ == 0 and past_len % TKC == 0
    assert cache_k.shape[3:] == (HEADS, HEAD_DIM) and d == HEADS * HEAD_DIM

    ng = norm_g[0].reshape(1, d)
    fg = final_g.reshape(1, d)
    pscale = pool_scale[0].reshape(1, d)
    win = w_in[0].astype(BF16)
    wpool = w_pool[0].astype(BF16)
    wa = w_br_a[0].astype(BF16)
    wb = w_br_b[0].astype(BF16)
    wout = w_out[0].astype(BF16)

    y_p, k_p, v_p, pool_p = _prompt_layer(x_prompt, ng, win, wpool, pscale, wa, wb, wout, fg)

    z = _sample_proj(x_sample.reshape(bs * ts, d), ng, win)
    hist = jnp.pad(state_pool[0], ((0, 0), (POOL_HIST - state_pool.shape[2], 0), (0, 0)))
    y_s, pool_s = _sample_layer(x_sample, z, cache_k.reshape(bs, past_len * HEADS, HEAD_DIM),
                                cache_v.reshape(bs, past_len * HEADS, HEAD_DIM), hist, wpool, pscale,
                                wa, wb, wout, fg)

    nbuf = state_pool.shape[2]
    hs = (HEADS, HEAD_DIM)
    return (y_p, y_s,
            k_p.reshape(1, b, t, *hs), v_p.reshape(1, b, t, *hs), pool_p[None, :, POOL_HIST - nbuf:],
            z[1].reshape(1, bs, ts, *hs), z[2].reshape(1, bs, ts, *hs), pool_s[None, :, POOL_HIST - nbuf:])
```

```python
import functools

import jax
import jax.numpy as jnp
from jax import lax
from jax.experimental import pallas as pl
from jax.experimental.pallas import tpu as pltpu

F32 = jnp.float32
BF16 = jnp.bfloat16

HEADS = 8
HEAD_DIM = 128
POOL_WINDOWS = (2, 4, 8, 16)
POOL_HIST = 16
EPS = 1e-6
SCALE = HEAD_DIM ** -0.5
LOG2E = 1.4426950408889634
USED_CAP = 88.0

TQ = 256
TKC = 256
VMEM_LIMIT = 56 * 1024 * 1024


def _rms(x, g):
    return x * lax.rsqrt(jnp.mean(x * x, axis=-1, keepdims=True) + EPS) * g


def _silu(x):
    return x * jax.nn.sigmoid(x)


def _tri(n):
    j = lax.broadcasted_iota(jnp.int32, (n, n), 0)
    s = lax.broadcasted_iota(jnp.int32, (n, n), 1)
    return (j > s).astype(BF16)


def _causal_mask(n):
    r = lax.broadcasted_iota(jnp.int32, (n, n), 0)
    c = lax.broadcasted_iota(jnp.int32, (n, n), 1)
    return c < r


def _head_rows(ref, lead, row0, rows, hd):
    return ref[lead, pl.ds(row0 * HEADS + hd, rows, stride=HEADS), :]


def _sb_block(q, k, v, tri, acc_ref, used_ref, mask, first):
    tk = k.shape[0]
    s = lax.dot_general(q, k, (((1,), (1,)), ((), ())), preferred_element_type=F32)
    l1p = jnp.log(1.0 + jnp.exp2(jnp.abs(s) * (-LOG2E)))
    neg_log_1m = jnp.maximum(s, 0.0) + l1p
    log_beta = s - neg_log_1m
    if mask is not None:
        neg_log_1m = jnp.where(mask, neg_log_1m, 0.0)
    later = jnp.dot(neg_log_1m.astype(BF16), tri, preferred_element_type=F32)
    if not first:
        used = used_ref[...]
        later = later + jnp.concatenate([used] * (tk // HEAD_DIM), axis=-1)
    a = jnp.exp(log_beta - later)
    if mask is not None:
        a = jnp.where(mask, a, 0.0)
    pv = jnp.dot(a.astype(BF16), v, preferred_element_type=F32)
    tot = jnp.broadcast_to(jnp.sum(neg_log_1m, axis=-1, keepdims=True), used_ref.shape)
    if first:
        acc_ref[...] = pv
        used_ref[...] = tot
    else:
        acc_ref[...] += pv
        used_ref[...] = used + tot


def _least_used(used_ref):
    m = used_ref[0]
    for hd in range(1, HEADS):
        m = jnp.minimum(m, used_ref[hd])
    return jnp.min(m)


def _pool(uext_ref, rows, pos0, wpool_ref, pscale_ref):
    n = POOL_HIST + rows
    gdim = uext_ref.shape[1] // len(POOL_WINDOWS)
    pos = pos0 + lax.broadcasted_iota(jnp.int32, (rows, 1), 0)
    outs = []
    for g, w in enumerate(POOL_WINDOWS):
        cols = slice(g * gdim, (g + 1) * gdim)
        e = uext_ref[:, cols]
        acc = e
        span = 1
        while span < w:
            acc = acc + pltpu.roll(acc, span, axis=0)
            span *= 2
        inv = 1.0 / jnp.minimum(pos + 1, w).astype(F32)
        p = acc[POOL_HIST:n] * inv - e[POOL_HIST:n]
        o = jnp.dot(p.astype(BF16), wpool_ref[g], preferred_element_type=F32)
        outs.append(o)
    return jnp.concatenate(outs, axis=-1) * pscale_ref[...]


def _merge(x, att_g, pb_g, gate_a, gate_b, wa_ref, wb_ref, wout_ref, fg):
    y_a = jnp.dot(att_g, wa_ref[...], preferred_element_type=F32)
    y_b = jnp.dot(pb_g, wb_ref[...], preferred_element_type=F32)
    m = gate_a * y_a + gate_b * y_b
    out = x + jnp.dot(m.astype(BF16), wout_ref[...], preferred_element_type=F32)
    return _rms(out, fg)


def _prompt_kernel(x_ref, ng_ref, win_ref, wpool_ref, pscale_ref, wa_ref, wb_ref, wout_ref, fg_ref,
                   y_ref, k_ref, v_ref, pool_ref,
                   kh_ref, vh_ref, q_ref, sga_ref, uext_ref, sgb_ref, gate_ref, attg_ref,
                   acc_ref, used_ref, tri_ref):
    t = pl.program_id(1)
    nt = pl.num_programs(1)
    d = x_ref.shape[2]
    row0 = pl.multiple_of(t * TQ, TQ)

    @pl.when(t == 0)
    def _():
        tri_ref[...] = _tri(TQ)
        uext_ref[0:POOL_HIST, :] = jnp.zeros((POOL_HIST, d), F32)

    @pl.when(t > 0)
    def _():
        uext_ref[0:POOL_HIST, :] = uext_ref[TQ:TQ + POOL_HIST, :]

    x = x_ref[0]
    h = _rms(x, ng_ref[...]).astype(BF16)

    def proj(c):
        return jnp.dot(h, win_ref[:, c * d:(c + 1) * d], preferred_element_type=F32)

    def head(z, hd):
        return z[:, hd * HEAD_DIM:(hd + 1) * HEAD_DIM]

    q = (proj(0) * SCALE).astype(BF16)
    for hd in range(HEADS):
        q_ref[hd] = head(q, hd)
    k = proj(1)
    kb = k.astype(BF16)
    for hd in range(HEADS):
        k_ref[0, pl.ds(hd, TQ, stride=HEADS), :] = head(k, hd)
        kh_ref[hd, pl.ds(row0, TQ), :] = head(kb, hd)
    v = proj(2)
    vb = v.astype(BF16)
    for hd in range(HEADS):
        v_ref[0, pl.ds(hd, TQ, stride=HEADS), :] = head(v, hd)
        vh_ref[hd, pl.ds(row0, TQ), :] = head(vb, hd)
    sga_ref[...] = _silu(proj(3)).astype(BF16)
    uext_ref[POOL_HIST:POOL_HIST + TQ, :] = proj(4)
    sgb_ref[...] = _silu(proj(5)).astype(BF16)
    gate_ref[:, 0:d] = jax.nn.sigmoid(proj(6))
    gate_ref[:, d:2 * d] = jax.nn.sigmoid(proj(7))

    @pl.when(t == nt - 1)
    def _():
        pool_ref[0] = uext_ref[TQ:TQ + POOL_HIST, :]

    def attend(r, mask, first):
        for hd in range(HEADS):
            _sb_block(q_ref[hd], kh_ref[hd, pl.ds(r, TQ), :], vh_ref[hd, pl.ds(r, TQ), :],
                      tri_ref[...], acc_ref.at[hd], used_ref.at[hd], mask, first)

    attend(row0, _causal_mask(TQ), True)

    def more(c):
        i, least_used = c
        return jnp.logical_and(i < t, least_used < USED_CAP)

    def body(c):
        i, _ = c
        attend(pl.multiple_of((t - 1 - i) * TQ, TQ), None, False)
        return i + 1, _least_used(used_ref)

    lax.while_loop(more, body, (jnp.int32(0), _least_used(used_ref)))
    for hd in range(HEADS):
        cols = slice(hd * HEAD_DIM, (hd + 1) * HEAD_DIM)
        attg_ref[:, cols] = (acc_ref[hd] * sga_ref[:, cols].astype(F32)).astype(BF16)

    pb = _pool(uext_ref, TQ, t * TQ, wpool_ref, pscale_ref)
    pb_g = (pb * sgb_ref[...].astype(F32)).astype(BF16)

    y_ref[0] = _merge(x_ref[0], attg_ref[...], pb_g, gate_ref[:, 0:d], gate_ref[:, d:2 * d],
                      wa_ref, wb_ref, wout_ref, fg_ref[...])


def _const_spec(shape):
    nd = len(shape)
    return pl.BlockSpec(shape, lambda *_: (0,) * nd, pipeline_mode=pl.Buffered(1))


def _prompt_layer(x, ng, win, wpool, pscale, wa, wb, wout, fg):
    b, t, d = x.shape
    nt = t // TQ
    tile = pl.BlockSpec((1, TQ, d), lambda i, j: (i, j, 0))
    kv_tile = pl.BlockSpec((1, TQ * HEADS, HEAD_DIM), lambda i, j: (i, j, 0))
    kv_shape = jax.ShapeDtypeStruct((b, t * HEADS, HEAD_DIM), F32)
    out_shape = (
        jax.ShapeDtypeStruct((b, t, d), F32),
        kv_shape,
        kv_shape,
        jax.ShapeDtypeStruct((b, POOL_HIST, d), F32),
    )
    return pl.pallas_call(
        _prompt_kernel,
        grid=(b, nt),
        in_specs=[tile, _const_spec(ng.shape), _const_spec(win.shape), _const_spec(wpool.shape),
                  _const_spec(pscale.shape), _const_spec(wa.shape), _const_spec(wb.shape),
                  _const_spec(wout.shape), _const_spec(fg.shape)],
        out_specs=(tile, kv_tile, kv_tile, pl.BlockSpec((1, POOL_HIST, d), lambda i, j: (i, 0, 0))),
        out_shape=out_shape,
        scratch_shapes=[
            pltpu.VMEM((HEADS, t, HEAD_DIM), BF16),
            pltpu.VMEM((HEADS, t, HEAD_DIM), BF16),
            pltpu.VMEM((HEADS, TQ, HEAD_DIM), BF16),
            pltpu.VMEM((TQ, d), BF16),
            pltpu.VMEM((POOL_HIST + TQ, d), F32),
            pltpu.VMEM((TQ, d), BF16),
            pltpu.VMEM((TQ, 2 * d), F32),
            pltpu.VMEM((TQ, d), BF16),
            pltpu.VMEM((HEADS, TQ, HEAD_DIM), F32),
            pltpu.VMEM((HEADS, TQ, HEAD_DIM), F32),
            pltpu.VMEM((TQ, TQ), BF16),
        ],
        compiler_params=pltpu.CompilerParams(
            dimension_semantics=("arbitrary", "arbitrary"), vmem_limit_bytes=VMEM_LIMIT),
        name="prompt_layer",
    )(x, ng, win, wpool, pscale, wa, wb, wout, fg)


def _sample_proj_kernel(x_ref, ng_ref, win_ref, z_ref):
    h = _rms(x_ref[...], ng_ref[...]).astype(BF16)
    z_ref[0] = jnp.dot(h, win_ref[...], preferred_element_type=F32)


def _sample_proj(x2, ng, win):
    n, d = x2.shape
    nc = win.shape[1] // d
    return pl.pallas_call(
        _sample_proj_kernel,
        grid=(nc,),
        in_specs=[pl.BlockSpec((n, d), lambda c: (0, 0)), pl.BlockSpec(ng.shape, lambda c: (0, 0)),
                  pl.BlockSpec((d, d), lambda c: (0, c))],
        out_specs=pl.BlockSpec((1, n, d), lambda c: (c, 0, 0)),
        out_shape=jax.ShapeDtypeStruct((nc, n, d), F32),
        compiler_params=pltpu.CompilerParams(dimension_semantics=("arbitrary",)),
        name="sample_proj",
    )(x2, ng, win)


def _sample_kernel(x_ref, z_ref, hist_ref, wpool_ref, pscale_ref, wa_ref, wb_ref, wout_ref, fg_ref,
                   ck_hbm, cv_hbm, y_ref, pool_ref,
                   acc_ref, used_ref, uext_ref, tri_ref, kbuf, vbuf, sem, *, past_len):
    b = pl.program_id(0)
    ts = x_ref.shape[1]
    nkb = past_len // TKC
    blk_rows = TKC * HEADS

    def cache_copies(c, slot):
        rows = pl.ds(pl.multiple_of(c * blk_rows, blk_rows), blk_rows)
        return (pltpu.make_async_copy(ck_hbm.at[b, rows], kbuf.at[slot], sem.at[0, slot]),
                pltpu.make_async_copy(cv_hbm.at[b, rows], vbuf.at[slot], sem.at[1, slot]))

    def start(c, slot):
        for cp in cache_copies(c, slot):
            cp.start()

    def wait(c, slot):
        for cp in cache_copies(c, slot):
            cp.wait()

    start(nkb - 1, 0)

    mask = _causal_mask(ts)
    tri_s = _tri(ts)
    for hd in range(HEADS):
        cols = slice(hd * HEAD_DIM, (hd + 1) * HEAD_DIM)
        _sb_block((z_ref[0, :, cols] * SCALE).astype(BF16), z_ref[1, :, cols].astype(BF16),
                  z_ref[2, :, cols].astype(BF16), tri_s, acc_ref.at[hd], used_ref.at[hd], mask, True)

    tri_ref[...] = _tri(TKC)

    def more(c):
        i, least_used = c
        return jnp.logical_and(i < nkb, least_used < USED_CAP)

    def body(c):
        i, _ = c
        slot = lax.rem(i, 2)
        wait(nkb - 1 - i, slot)

        @pl.when(i + 1 < nkb)
        def _():
            start(nkb - 2 - i, 1 - slot)

        for hd in range(HEADS):
            cols = slice(hd * HEAD_DIM, (hd + 1) * HEAD_DIM)
            _sb_block((z_ref[0, :, cols] * SCALE).astype(BF16),
                      _head_rows(kbuf, slot, 0, TKC, hd).astype(BF16),
                      _head_rows(vbuf, slot, 0, TKC, hd).astype(BF16),
                      tri_ref[...], acc_ref.at[hd], used_ref.at[hd], None, False)
        return i + 1, _least_used(used_ref)

    done, _ = lax.while_loop(more, body, (jnp.int32(0), _least_used(used_ref)))

    @pl.when(done < nkb)
    def _():
        wait(nkb - 1 - done, lax.rem(done, 2))

    att = jnp.concatenate([acc_ref[hd] for hd in range(HEADS)], axis=-1)
    att_g = (att * _silu(z_ref[3])).astype(BF16)
    uext_ref[0:POOL_HIST, :] = hist_ref[0]
    uext_ref[POOL_HIST:POOL_HIST + ts, :] = z_ref[4]
    pool_ref[0] = uext_ref[ts:ts + POOL_HIST, :]
    pb = _pool(uext_ref, ts, past_len, wpool_ref, pscale_ref)
    pb_g = (pb * _silu(z_ref[5])).astype(BF16)
    y_ref[0] = _merge(x_ref[0], att_g, pb_g, jax.nn.sigmoid(z_ref[6]), jax.nn.sigmoid(z_ref[7]),
                      wa_ref, wb_ref, wout_ref, fg_ref[...])


def _sample_layer(x, z, ck, cv, hist, wpool, pscale, wa, wb, wout, fg):
    b, ts, d = x.shape
    past_len = ck.shape[1] // HEADS
    nc = z.shape[0]
    hbm = pl.BlockSpec(memory_space=pl.ANY)
    return pl.pallas_call(
        functools.partial(_sample_kernel, past_len=past_len),
        grid=(b,),
        in_specs=[pl.BlockSpec((1, ts, d), lambda i: (i, 0, 0)),
                  pl.BlockSpec((nc, ts, d), lambda i: (0, i, 0)),
                  pl.BlockSpec((1, POOL_HIST, d), lambda i: (i, 0, 0)),
                  _const_spec(wpool.shape), _const_spec(pscale.shape), _const_spec(wa.shape),
                  _const_spec(wb.shape), _const_spec(wout.shape), _const_spec(fg.shape),
                  hbm, hbm],
        out_specs=(pl.BlockSpec((1, ts, d), lambda i: (i, 0, 0)),
                   pl.BlockSpec((1, POOL_HIST, d), lambda i: (i, 0, 0))),
        out_shape=(jax.ShapeDtypeStruct((b, ts, d), F32),
                   jax.ShapeDtypeStruct((b, POOL_HIST, d), F32)),
        scratch_shapes=[
            pltpu.VMEM((HEADS, ts, HEAD_DIM), F32),
            pltpu.VMEM((HEADS, ts, HEAD_DIM), F32),
            pltpu.VMEM((POOL_HIST + ts, d), F32),
            pltpu.VMEM((TKC, TKC), BF16),
            pltpu.VMEM((2, TKC * HEADS, HEAD_DIM), F32),
            pltpu.VMEM((2, TKC * HEADS, HEAD_DIM), F32),
            pltpu.SemaphoreType.DMA((2, 2)),
        ],
        compiler_params=pltpu.CompilerParams(
            dimension_semantics=("arbitrary",), vmem_limit_bytes=VMEM_LIMIT),
        name="sample_layer",
    )(x, z, hist, wpool, pscale, wa, wb, wout, fg, ck, cv)


def kernel(x_prompt, x_sample, cache_k, cache_v, state_pool, norm_g, w_in, w_pool, pool_scale,
           w_br_a, w_br_b, w_out, final_g):
    depth = norm_g.shape[0]
    assert depth == 1, "single-layer stack"
    b, t, d = x_prompt.shape
    bs, ts, _ = x_sample.shape
    past_len = cache_k.shape[2]
    assert t % TQ == 0 and past_len % TKC == 0
    assert cache_k.shape[3:] == (HEADS, HEAD_DIM) and d == HEADS * HEAD_DIM

    ng = norm_g[0].reshape(1, d)
    fg = final_g.reshape(1, d)
    pscale = pool_scale[0].reshape(1, d)
    win = w_in[0].astype(BF16)
    wpool = w_pool[0].astype(BF16)
    wa = w_br_a[0].astype(BF16)
    wb = w_br_b[0].astype(BF16)
    wout = w_out[0].astype(BF16)

    y_p, k_p, v_p, pool_p = _prompt_layer(x_prompt, ng, win, wpool, pscale, wa, wb, wout, fg)

    z = _sample_proj(x_sample.reshape(bs * ts, d), ng, win)
    hist = jnp.pad(state_pool[0], ((0, 0), (POOL_HIST - state_pool.shape[2], 0), (0, 0)))
    y_s, pool_s = _sample_layer(x_sample, z, cache_k.reshape(bs, past_len * HEADS, HEAD_DIM),
                                cache_v.reshape(bs, past_len * HEADS, HEAD_DIM), hist, wpool, pscale,
                                wa, wb, wout, fg)

    nbuf = state_pool.shape[2]
    hs = (HEADS, HEAD_DIM)
    return (y_p, y_s,
            k_p.reshape(1, b, t, *hs), v_p.reshape(1, b, t, *hs), pool_p[None, :, POOL_HIST - nbuf:],
            z[1].reshape(1, bs, ts, *hs), z[2].reshape(1, bs, ts, *hs), pool_s[None, :, POOL_HIST - nbuf:])
```

```python
import functools

import jax
import jax.numpy as jnp
from jax import lax
from jax.experimental import pallas as pl
from jax.experimental.pallas import tpu as pltpu

F32 = jnp.float32
BF16 = jnp.bfloat16

HEADS = 8
HEAD_DIM = 128
POOL_WINDOWS = (2, 4, 8, 16)
POOL_HIST = 16
EPS = 1e-6
SCALE = HEAD_DIM ** -0.5
LOG2E = 1.4426950408889634
USED_CAP = 88.0
USED_UP = 1e30

TQ = 256
TKC = 256
HEAD_GROUP = 4
VMEM_LIMIT = 56 * 1024 * 1024


def _rms(x, g):
    return x * lax.rsqrt(jnp.mean(x * x, axis=-1, keepdims=True) + EPS) * g


def _silu(x):
    return x * jax.nn.sigmoid(x)


def _tri(n):
    j = lax.broadcasted_iota(jnp.int32, (n, n), 0)
    s = lax.broadcasted_iota(jnp.int32, (n, n), 1)
    return (j > s).astype(BF16)


def _causal_mask(n):
    r = lax.broadcasted_iota(jnp.int32, (n, n), 0)
    c = lax.broadcasted_iota(jnp.int32, (n, n), 1)
    return c < r


def _head_rows(ref, lead, row0, rows, hd):
    return ref[lead, pl.ds(row0 * HEADS + hd, rows, stride=HEADS), :]


def _sb_block(qs, ks, vs, tri, acc_ref, used_ref, mask, used0, fillers=None):
    first = used0 is not None
    def fill():
        thunk = next(fillers, None) if fillers is not None else None
        if thunk is not None:
            thunk()

    if len(qs) > HEAD_GROUP:
        for g in range(0, len(qs), HEAD_GROUP):
            grp = slice(g, g + HEAD_GROUP)
            _sb_block(qs[grp], ks[grp], vs[grp], tri, acc_ref.at[grp], used_ref.at[grp], mask, used0,
                      fillers)
        return
    nh = len(qs)
    tk = ks[0].shape[0]
    stage1 = []
    for hd in range(nh):
        s = lax.dot_general(qs[hd], ks[hd], (((1,), (1,)), ((), ())), preferred_element_type=F32)
        l1p = jnp.log(1.0 + jnp.exp2(jnp.abs(s) * (-LOG2E)))
        neg_log_1m = jnp.maximum(s, 0.0) + l1p
        log_beta = s - neg_log_1m
        if mask is not None:
            neg_log_1m = jnp.where(mask, neg_log_1m, 0.0)
        tot = jnp.broadcast_to(jnp.sum(neg_log_1m, axis=-1, keepdims=True), acc_ref.shape[1:])
        stage1.append((neg_log_1m.astype(BF16), log_beta, tot))
    fill()
    stage2 = []
    for hd in range(nh):
        neg_log_1m, log_beta, tot = stage1[hd]
        later = jnp.dot(neg_log_1m, tri, preferred_element_type=F32)
        if first:
            used_ref[hd] = tot + used0
        else:
            used = used_ref[hd]
            later = later + jnp.concatenate([used] * (tk // HEAD_DIM), axis=-1)
            used_ref[hd] = used + tot
        a = jnp.exp(log_beta - later)
        if mask is not None:
            a = jnp.where(mask, a, 0.0)
        stage2.append(a.astype(BF16))
    fill()
    for hd in range(nh):
        pv = jnp.dot(stage2[hd], vs[hd], preferred_element_type=F32)
        if first:
            acc_ref[hd] = pv
        else:
            acc_ref[hd] += pv
    fill()


def _least_used(used_ref):
    m = used_ref[0]
    for hd in range(1, HEADS):
        m = jnp.minimum(m, used_ref[hd])
    return jnp.min(m)


def _pool(uext_ref, rows, pos0, wpool_ref, pscale_ref):
    n = POOL_HIST + rows
    gdim = uext_ref.shape[1] // len(POOL_WINDOWS)
    pos = pos0 + lax.broadcasted_iota(jnp.int32, (rows, 1), 0)
    outs = []
    for g, w in enumerate(POOL_WINDOWS):
        cols = slice(g * gdim, (g + 1) * gdim)
        e = uext_ref[:, cols]
        acc = e
        span = 1
        while span < w:
            acc = acc + pltpu.roll(acc, span, axis=0)
            span *= 2
        inv = 1.0 / jnp.minimum(pos + 1, w).astype(F32)
        p = acc[POOL_HIST:n] * inv - e[POOL_HIST:n]
        o = jnp.dot(p.astype(BF16), wpool_ref[g], preferred_element_type=F32)
        outs.append(o)
    return jnp.concatenate(outs, axis=-1) * pscale_ref[...]


def _merge(x, att_g, y_b, gate_a, gate_b, wa_ref, wout_ref, fg):
    y_a = jnp.dot(att_g, wa_ref[...], preferred_element_type=F32)
    m = gate_a * y_a + gate_b * y_b
    out = x + jnp.dot(m.astype(BF16), wout_ref[...], preferred_element_type=F32)
    return _rms(out, fg)


def _prompt_kernel(x_ref, ng_ref, win_ref, wpool_ref, pscale_ref, wa_ref, wb_ref, wout_ref, fg_ref,
                   y_ref, k_ref, v_ref, pool_ref,
                   kh_ref, vh_ref, q_ref, sga_ref, uext_ref, sgb_ref, gate_ref, attg_ref,
                   acc_ref, used_ref, tri_ref, yb_ref):
    t = pl.program_id(1)
    nt = pl.num_programs(1)
    d = x_ref.shape[2]
    row0 = pl.multiple_of(t * TQ, TQ)

    @pl.when(t == 0)
    def _():
        tri_ref[...] = _tri(TQ)
        uext_ref[0:POOL_HIST, :] = jnp.zeros((POOL_HIST, d), F32)

    @pl.when(t > 0)
    def _():
        uext_ref[0:POOL_HIST, :] = uext_ref[TQ:TQ + POOL_HIST, :]

    x = x_ref[0]
    h = _rms(x, ng_ref[...]).astype(BF16)

    def proj(c):
        return jnp.dot(h, win_ref[:, c * d:(c + 1) * d], preferred_element_type=F32)

    def head(z, hd):
        return z[:, hd * HEAD_DIM:(hd + 1) * HEAD_DIM]

    q = (proj(0) * SCALE).astype(BF16)
    for hd in range(HEADS):
        q_ref[hd] = head(q, hd)
    k = proj(1)
    kb = k.astype(BF16)
    for hd in range(HEADS):
        k_ref[0, pl.ds(hd, TQ, stride=HEADS), :] = head(k, hd)
        kh_ref[hd, pl.ds(row0, TQ), :] = head(kb, hd)
    v = proj(2)
    vb = v.astype(BF16)
    for hd in range(HEADS):
        v_ref[0, pl.ds(hd, TQ, stride=HEADS), :] = head(v, hd)
        vh_ref[hd, pl.ds(row0, TQ), :] = head(vb, hd)
    def proj_ga():
        sga_ref[...] = _silu(proj(3)).astype(BF16)

    def proj_u():
        uext_ref[POOL_HIST:POOL_HIST + TQ, :] = proj(4)

    def proj_gb():
        sgb_ref[...] = _silu(proj(5)).astype(BF16)

    def proj_gate_a():
        gate_ref[:, 0:d] = jax.nn.sigmoid(proj(6))

    def proj_gate_b():
        gate_ref[:, d:2 * d] = jax.nn.sigmoid(proj(7))

    def branch_b():
        pb = _pool(uext_ref, TQ, t * TQ, wpool_ref, pscale_ref)
        pb_g = (pb * sgb_ref[...].astype(F32)).astype(BF16)
        yb_ref[...] = jnp.dot(pb_g, wb_ref[...], preferred_element_type=F32)

    heads = range(HEADS)
    used0 = jnp.where(t == 0, USED_UP, 0.0)
    fillers = iter((proj_ga, proj_u, proj_gb, proj_gate_a, proj_gate_b, branch_b))
    _sb_block([head(q, hd) for hd in heads], [head(kb, hd) for hd in heads],
              [head(vb, hd) for hd in heads], tri_ref[...], acc_ref, used_ref, _causal_mask(TQ), used0,
              fillers)

    def earlier_block(i, fill=None):
        r = pl.multiple_of(jnp.maximum(t - 1 - i, 0) * TQ, TQ)
        _sb_block([q_ref[hd] for hd in heads], [kh_ref[hd, pl.ds(r, TQ), :] for hd in heads],
                  [vh_ref[hd, pl.ds(r, TQ), :] for hd in heads], tri_ref[...], acc_ref, used_ref,
                  None, None, fill)

    earlier_block(0, fillers)
    for thunk in fillers:
        if thunk is not None:
            thunk()

    def more(c):
        i, least_used = c
        return jnp.logical_and(i < t, least_used < USED_CAP)

    def body(c):
        i, _ = c
        earlier_block(i)
        return i + 1, _least_used(used_ref)

    lax.while_loop(more, body, (jnp.int32(1), _least_used(used_ref)))
    for hd in range(HEADS):
        cols = slice(hd * HEAD_DIM, (hd + 1) * HEAD_DIM)
        attg_ref[:, cols] = (acc_ref[hd] * sga_ref[:, cols].astype(F32)).astype(BF16)

    y_ref[0] = _merge(x_ref[0], attg_ref[...], yb_ref[...], gate_ref[:, 0:d], gate_ref[:, d:2 * d],
                      wa_ref, wout_ref, fg_ref[...])

    @pl.when(t == nt - 1)
    def _():
        pool_ref[0] = uext_ref[TQ:TQ + POOL_HIST, :]


def _const_spec(shape):
    nd = len(shape)
    return pl.BlockSpec(shape, lambda *_: (0,) * nd, pipeline_mode=pl.Buffered(1))


def _prompt_layer(x, ng, win, wpool, pscale, wa, wb, wout, fg):
    b, t, d = x.shape
    nt = t // TQ
    tile = pl.BlockSpec((1, TQ, d), lambda i, j: (i, j, 0))
    kv_tile = pl.BlockSpec((1, TQ * HEADS, HEAD_DIM), lambda i, j: (i, j, 0))
    kv_shape = jax.ShapeDtypeStruct((b, t * HEADS, HEAD_DIM), F32)
    out_shape = (
        jax.ShapeDtypeStruct((b, t, d), F32),
        kv_shape,
        kv_shape,
        jax.ShapeDtypeStruct((b, POOL_HIST, d), F32),
    )
    return pl.pallas_call(
        _prompt_kernel,
        grid=(b, nt),
        in_specs=[tile, _const_spec(ng.shape), _const_spec(win.shape), _const_spec(wpool.shape),
                  _const_spec(pscale.shape), _const_spec(wa.shape), _const_spec(wb.shape),
                  _const_spec(wout.shape), _const_spec(fg.shape)],
        out_specs=(tile, kv_tile, kv_tile, pl.BlockSpec((1, POOL_HIST, d), lambda i, j: (i, 0, 0))),
        out_shape=out_shape,
        scratch_shapes=[
            pltpu.VMEM((HEADS, t, HEAD_DIM), BF16),
            pltpu.VMEM((HEADS, t, HEAD_DIM), BF16),
            pltpu.VMEM((HEADS, TQ, HEAD_DIM), BF16),
            pltpu.VMEM((TQ, d), BF16),
            pltpu.VMEM((POOL_HIST + TQ, d), F32),
            pltpu.VMEM((TQ, d), BF16),
            pltpu.VMEM((TQ, 2 * d), F32),
            pltpu.VMEM((TQ, d), BF16),
            pltpu.VMEM((HEADS, TQ, HEAD_DIM), F32),
            pltpu.VMEM((HEADS, TQ, HEAD_DIM), F32),
            pltpu.VMEM((TQ, TQ), BF16),
            pltpu.VMEM((TQ, d), F32),
        ],
        compiler_params=pltpu.CompilerParams(
            dimension_semantics=("arbitrary", "arbitrary"), vmem_limit_bytes=VMEM_LIMIT),
        name="prompt_layer",
    )(x, ng, win, wpool, pscale, wa, wb, wout, fg)


def _sample_proj_kernel(x_ref, ng_ref, win_ref, z_ref):
    h = _rms(x_ref[...], ng_ref[...]).astype(BF16)
    z_ref[0] = jnp.dot(h, win_ref[...], preferred_element_type=F32)


def _sample_proj(x2, ng, win):
    n, d = x2.shape
    nc = win.shape[1] // d
    return pl.pallas_call(
        _sample_proj_kernel,
        grid=(nc,),
        in_specs=[pl.BlockSpec((n, d), lambda c: (0, 0)), pl.BlockSpec(ng.shape, lambda c: (0, 0)),
                  pl.BlockSpec((d, d), lambda c: (0, c))],
        out_specs=pl.BlockSpec((1, n, d), lambda c: (c, 0, 0)),
        out_shape=jax.ShapeDtypeStruct((nc, n, d), F32),
        compiler_params=pltpu.CompilerParams(dimension_semantics=("arbitrary",)),
        name="sample_proj",
    )(x2, ng, win)


def _sample_kernel(x_ref, z_ref, hist_ref, wpool_ref, pscale_ref, wa_ref, wb_ref, wout_ref, fg_ref,
                   ck_hbm, cv_hbm, y_ref, pool_ref,
                   acc_ref, used_ref, uext_ref, tri_ref, kbuf, vbuf, sem, *, past_len):
    b = pl.program_id(0)
    ts = x_ref.shape[1]
    nkb = past_len // TKC
    blk_rows = TKC * HEADS

    def cache_copies(c, slot):
        rows = pl.ds(pl.multiple_of(c * blk_rows, blk_rows), blk_rows)
        return (pltpu.make_async_copy(ck_hbm.at[b, rows], kbuf.at[slot], sem.at[0, slot]),
                pltpu.make_async_copy(cv_hbm.at[b, rows], vbuf.at[slot], sem.at[1, slot]))

    def start(c, slot):
        for cp in cache_copies(c, slot):
            cp.start()

    def wait(c, slot):
        for cp in cache_copies(c, slot):
            cp.wait()

    start(nkb - 1, 0)

    heads = range(HEADS)

    def chunk_heads(c, scale=None):
        z = z_ref[c] if scale is None else z_ref[c] * scale
        z = z.astype(BF16)
        return [z[:, hd * HEAD_DIM:(hd + 1) * HEAD_DIM] for hd in heads]

    qs = chunk_heads(0, SCALE)
    _sb_block(qs, chunk_heads(1), chunk_heads(2), _tri(ts), acc_ref, used_ref, _causal_mask(ts), 0.0)

    tri_ref[...] = _tri(TKC)

    def more(c):
        i, least_used = c
        return jnp.logical_and(i < nkb, least_used < USED_CAP)

    def body(c):
        i, _ = c
        slot = lax.rem(i, 2)
        wait(nkb - 1 - i, slot)

        @pl.when(i + 1 < nkb)
        def _():
            start(nkb - 2 - i, 1 - slot)

        _sb_block(chunk_heads(0, SCALE),
                  [_head_rows(kbuf, slot, 0, TKC, hd).astype(BF16) for hd in heads],
                  [_head_rows(vbuf, slot, 0, TKC, hd).astype(BF16) for hd in heads],
                  tri_ref[...], acc_ref, used_ref, None, None)
        return i + 1, _least_used(used_ref)

    done, _ = lax.while_loop(more, body, (jnp.int32(0), _least_used(used_ref)))

    @pl.when(done < nkb)
    def _():
        wait(nkb - 1 - done, lax.rem(done, 2))

    att = jnp.concatenate([acc_ref[hd] for hd in range(HEADS)], axis=-1)
    att_g = (att * _silu(z_ref[3])).astype(BF16)
    uext_ref[0:POOL_HIST, :] = hist_ref[0]
    uext_ref[POOL_HIST:POOL_HIST + ts, :] = z_ref[4]
    pool_ref[0] = uext_ref[ts:ts + POOL_HIST, :]
    pb = _pool(uext_ref, ts, past_len, wpool_ref, pscale_ref)
    pb_g = (pb * _silu(z_ref[5])).astype(BF16)
    y_b = jnp.dot(pb_g, wb_ref[...], preferred_element_type=F32)
    y_ref[0] = _merge(x_ref[0], att_g, y_b, jax.nn.sigmoid(z_ref[6]), jax.nn.sigmoid(z_ref[7]),
                      wa_ref, wout_ref, fg_ref[...])


def _sample_layer(x, z, ck, cv, hist, wpool, pscale, wa, wb, wout, fg):
    b, ts, d = x.shape
    past_len = ck.shape[1] // HEADS
    nc = z.shape[0]
    hbm = pl.BlockSpec(memory_space=pl.ANY)
    return pl.pallas_call(
        functools.partial(_sample_kernel, past_len=past_len),
        grid=(b,),
        in_specs=[pl.BlockSpec((1, ts, d), lambda i: (i, 0, 0)),
                  pl.BlockSpec((nc, ts, d), lambda i: (0, i, 0)),
                  pl.BlockSpec((1, POOL_HIST, d), lambda i: (i, 0, 0)),
                  _const_spec(wpool.shape), _const_spec(pscale.shape), _const_spec(wa.shape),
                  _const_spec(wb.shape), _const_spec(wout.shape), _const_spec(fg.shape),
                  hbm, hbm],
        out_specs=(pl.BlockSpec((1, ts, d), lambda i: (i, 0, 0)),
                   pl.BlockSpec((1, POOL_HIST, d), lambda i: (i, 0, 0))),
        out_shape=(jax.ShapeDtypeStruct((b, ts, d), F32),
                   jax.ShapeDtypeStruct((b, POOL_HIST, d), F32)),
        scratch_shapes=[
            pltpu.VMEM((HEADS, ts, HEAD_DIM), F32),
            pltpu.VMEM((HEADS, ts, HEAD_DIM), F32),
            pltpu.VMEM((POOL_HIST + ts, d), F32),
            pltpu.VMEM((TKC, TKC), BF16),
            pltpu.VMEM((2, TKC * HEADS, HEAD_DIM), F32),
            pltpu.VMEM((2, TKC * HEADS, HEAD_DIM), F32),
            pltpu.SemaphoreType.DMA((2, 2)),
        ],
        compiler_params=pltpu.CompilerParams(
            dimension_semantics=("arbitrary",), vmem_limit_bytes=VMEM_LIMIT),
        name="sample_layer",
    )(x, z, hist, wpool, pscale, wa, wb, wout, fg, ck, cv)


def kernel(x_prompt, x_sample, cache_k, cache_v, state_pool, norm_g, w_in, w_pool, pool_scale,
           w_br_a, w_br_b, w_out, final_g):
    depth = norm_g.shape[0]
    assert depth == 1, "single-layer stack"
    b, t, d = x_prompt.shape
    bs, ts, _ = x_sample.shape
    past_len = cache_k.shape[2]
    assert t % TQ == 0 and past_len % TKC == 0
    assert cache_k.shape[3:] == (HEADS, HEAD_DIM) and d == HEADS * HEAD_DIM

    ng = norm_g[0].reshape(1, d)
    fg = final_g.reshape(1, d)
    pscale = pool_scale[0].reshape(1, d)
    win = w_in[0].astype(BF16)
    wpool = w_pool[0].astype(BF16)
    wa = w_br_a[0].astype(BF16)
    wb = w_br_b[0].astype(BF16)
    wout = w_out[0].astype(BF16)

    y_p, k_p, v_p, pool_p = _prompt_layer(x_prompt, ng, win, wpool, pscale, wa, wb, wout, fg)

    z = _sample_proj(x_sample.reshape(bs * ts, d), ng, win)
    hist = jnp.pad(state_pool[0], ((0, 0), (POOL_HIST - state_pool.shape[2], 0), (0, 0)))
    y_s, pool_s = _sample_layer(x_sample, z, cache_k.reshape(bs, past_len * HEADS, HEAD_DIM),
                                cache_v.reshape(bs, past_len * HEADS, HEAD_DIM), hist, wpool, pscale,
                                wa, wb, wout, fg)

    nbuf = state_pool.shape[2]
    hs = (HEADS, HEAD_DIM)
    return (y_p, y_s,
            k_p.reshape(1, b, t, *hs), v_p.reshape(1, b, t, *hs), pool_p[None, :, POOL_HIST - nbuf:],
            z[1].reshape(1, bs, ts, *hs), z[2].reshape(1, bs, ts, *hs), pool_s[None, :, POOL_HIST - nbuf:])
```

```python
import functools

import jax
import jax.numpy as jnp
from jax import lax
from jax.experimental import pallas as pl
from jax.experimental.pallas import tpu as pltpu

F32 = jnp.float32
BF16 = jnp.bfloat16

HEADS = 8
HEAD_DIM = 128
POOL_WINDOWS = (2, 4, 8, 16)
POOL_HIST = 16
EPS = 1e-6
SCALE = HEAD_DIM ** -0.5
LOG2E = 1.4426950408889634
USED_CAP = 88.0
USED_UP = 1e30

TQ = 256
TKC = 256
HEAD_GROUP = 4
VMEM_LIMIT = 56 * 1024 * 1024


def _rms(x, g):
    return x * lax.rsqrt(jnp.mean(x * x, axis=-1, keepdims=True) + EPS) * g


def _silu(x):
    return x * jax.nn.sigmoid(x)


def _tri(n):
    j = lax.broadcasted_iota(jnp.int32, (n, n), 0)
    s = lax.broadcasted_iota(jnp.int32, (n, n), 1)
    return (j > s).astype(BF16)


def _causal_mask(n):
    r = lax.broadcasted_iota(jnp.int32, (n, n), 0)
    c = lax.broadcasted_iota(jnp.int32, (n, n), 1)
    return c < r


def _head_rows(ref, lead, row0, rows, hd):
    return ref[lead, pl.ds(row0 * HEADS + hd, rows, stride=HEADS), :]


def _sb_block(qs, ks, vs, tri, acc_ref, used_ref, causal, used0, fillers=None):
    first = used0 is not None
    def fill():
        thunk = next(fillers, None) if fillers is not None else None
        if thunk is not None:
            thunk()

    if len(qs) > HEAD_GROUP:
        for g in range(0, len(qs), HEAD_GROUP):
            grp = slice(g, g + HEAD_GROUP)
            _sb_block(qs[grp], ks[grp], vs[grp], tri, acc_ref.at[grp], used_ref.at[grp], causal, used0,
                      fillers)
        return
    nh = len(qs)
    tk = ks[0].shape[0]
    mask = _causal_mask(tk) if causal else None
    stage1 = []
    for hd in range(nh):
        s = lax.dot_general(qs[hd], ks[hd], (((1,), (1,)), ((), ())), preferred_element_type=F32)
        l1p = jnp.log(1.0 + jnp.exp2(jnp.abs(s) * (-LOG2E)))
        neg_log_1m = jnp.maximum(s, 0.0) + l1p
        log_beta = s - neg_log_1m
        if mask is not None:
            neg_log_1m = jnp.where(mask, neg_log_1m, 0.0)
        tot = jnp.broadcast_to(jnp.sum(neg_log_1m, axis=-1, keepdims=True), acc_ref.shape[1:])
        stage1.append((neg_log_1m.astype(BF16), log_beta, tot))
    fill()
    stage2 = []
    for hd in range(nh):
        neg_log_1m, log_beta, tot = stage1[hd]
        later = jnp.dot(neg_log_1m, tri, preferred_element_type=F32)
        if first:
            used_ref[hd] = tot + used0
        else:
            used = used_ref[hd]
            later = later + jnp.concatenate([used] * (tk // HEAD_DIM), axis=-1)
            used_ref[hd] = used + tot
        a = jnp.exp(log_beta - later)
        if mask is not None:
            a = jnp.where(mask, a, 0.0)
        stage2.append(a.astype(BF16))
    fill()
    for hd in range(nh):
        pv = jnp.dot(stage2[hd], vs[hd], preferred_element_type=F32)
        if first:
            acc_ref[hd] = pv
        else:
            acc_ref[hd] += pv
    fill()


def _least_used(used_ref):
    m = used_ref[0]
    for hd in range(1, HEADS):
        m = jnp.minimum(m, used_ref[hd])
    return jnp.min(m)


def _pool(uext_ref, rows, pos0, wpool_ref, pscale_ref):
    n = POOL_HIST + rows
    gdim = uext_ref.shape[1] // len(POOL_WINDOWS)
    pos = pos0 + lax.broadcasted_iota(jnp.int32, (rows, 1), 0)
    outs = []
    for g, w in enumerate(POOL_WINDOWS):
        cols = slice(g * gdim, (g + 1) * gdim)
        e = uext_ref[:, cols]
        acc = e
        span = 1
        while span < w:
            acc = acc + pltpu.roll(acc, span, axis=0)
            span *= 2
        inv = 1.0 / jnp.minimum(pos + 1, w).astype(F32)
        p = acc[POOL_HIST:n] * inv - e[POOL_HIST:n]
        o = jnp.dot(p.astype(BF16), wpool_ref[g], preferred_element_type=F32)
        outs.append(o)
    return jnp.concatenate(outs, axis=-1) * pscale_ref[...]


def _merge(x, att_g, y_b, gate_a, gate_b, wa_ref, wout_ref, fg):
    y_a = jnp.dot(att_g, wa_ref[...], preferred_element_type=F32)
    m = gate_a * y_a + gate_b * y_b
    out = x + jnp.dot(m.astype(BF16), wout_ref[...], preferred_element_type=F32)
    return _rms(out, fg)


def _prompt_kernel(x_ref, ng_ref, win_ref, wpool_ref, pscale_ref, wa_ref, wb_ref, wout_ref, fg_ref,
                   y_ref, k_ref, v_ref, pool_ref,
                   kh_ref, vh_ref, q_ref, sga_ref, uext_ref, sgb_ref, gate_ref, attg_ref,
                   acc_ref, used_ref, tri_ref, yb_ref):
    t = pl.program_id(1)
    nt = pl.num_programs(1)
    d = x_ref.shape[2]
    row0 = pl.multiple_of(t * TQ, TQ)

    @pl.when(t == 0)
    def _():
        tri_ref[...] = _tri(TQ)
        uext_ref[0:POOL_HIST, :] = jnp.zeros((POOL_HIST, d), F32)

    @pl.when(t > 0)
    def _():
        uext_ref[0:POOL_HIST, :] = uext_ref[TQ:TQ + POOL_HIST, :]

    x = x_ref[0]
    h = _rms(x, ng_ref[...]).astype(BF16)

    def proj(c):
        return jnp.dot(h, win_ref[:, c * d:(c + 1) * d], preferred_element_type=F32)

    def head(z, hd):
        return z[:, hd * HEAD_DIM:(hd + 1) * HEAD_DIM]

    q = (proj(0) * SCALE).astype(BF16)
    for hd in range(HEADS):
        q_ref[hd] = head(q, hd)
    k = proj(1)
    kb = k.astype(BF16)
    for hd in range(HEADS):
        k_ref[0, pl.ds(hd, TQ, stride=HEADS), :] = head(k, hd)
        kh_ref[hd, pl.ds(row0, TQ), :] = head(kb, hd)
    v = proj(2)
    vb = v.astype(BF16)
    for hd in range(HEADS):
        v_ref[0, pl.ds(hd, TQ, stride=HEADS), :] = head(v, hd)
        vh_ref[hd, pl.ds(row0, TQ), :] = head(vb, hd)
    def proj_ga():
        sga_ref[...] = _silu(proj(3)).astype(BF16)

    def proj_u():
        uext_ref[POOL_HIST:POOL_HIST + TQ, :] = proj(4)

    def proj_gb():
        sgb_ref[...] = _silu(proj(5)).astype(BF16)

    def proj_gate_a():
        gate_ref[:, 0:d] = jax.nn.sigmoid(proj(6))

    def proj_gate_b():
        gate_ref[:, d:2 * d] = jax.nn.sigmoid(proj(7))

    def branch_b():
        pb = _pool(uext_ref, TQ, t * TQ, wpool_ref, pscale_ref)
        pb_g = (pb * sgb_ref[...].astype(F32)).astype(BF16)
        yb_ref[...] = jnp.dot(pb_g, wb_ref[...], preferred_element_type=F32)

    heads = range(HEADS)
    used0 = jnp.where(t == 0, USED_UP, 0.0)
    fillers = iter((proj_ga, proj_u, proj_gb, proj_gate_a, proj_gate_b, branch_b))
    _sb_block([head(q, hd) for hd in heads], [head(kb, hd) for hd in heads],
              [head(vb, hd) for hd in heads], tri_ref[...], acc_ref, used_ref, True, used0,
              fillers)

    def earlier_block(i, rows, fill=None):
        r = pl.multiple_of(jnp.maximum(t - 1 - i, 0) * TQ, TQ)
        _sb_block([q_ref[hd, rows, :] for hd in heads], [kh_ref[hd, pl.ds(r, TQ), :] for hd in heads],
                  [vh_ref[hd, pl.ds(r, TQ), :] for hd in heads], tri_ref[...],
                  acc_ref.at[:, rows], used_ref.at[:, rows], False, None, fill)

    def earlier_blocks_while_needed(first_i, rows):
        def more(c):
            i, least_used = c
            return jnp.logical_and(i < t, least_used < USED_CAP)

        def body(c):
            i, _ = c
            earlier_block(i, rows)
            return i + 1, _least_used(used_ref.at[:, rows])

        lax.while_loop(more, body, (jnp.int32(first_i), _least_used(used_ref.at[:, rows])))

    top, bottom = slice(0, TQ // 2), slice(TQ // 2, TQ)
    earlier_block(0, top, fillers)
    for thunk in fillers:
        if thunk is not None:
            thunk()
    earlier_blocks_while_needed(0, bottom)
    earlier_blocks_while_needed(1, top)
    for hd in range(HEADS):
        cols = slice(hd * HEAD_DIM, (hd + 1) * HEAD_DIM)
        attg_ref[:, cols] = (acc_ref[hd] * sga_ref[:, cols].astype(F32)).astype(BF16)

    y_ref[0] = _merge(x_ref[0], attg_ref[...], yb_ref[...], gate_ref[:, 0:d], gate_ref[:, d:2 * d],
                      wa_ref, wout_ref, fg_ref[...])

    @pl.when(t == nt - 1)
    def _():
        pool_ref[0] = uext_ref[TQ:TQ + POOL_HIST, :]


def _const_spec(shape):
    nd = len(shape)
    return pl.BlockSpec(shape, lambda *_: (0,) * nd, pipeline_mode=pl.Buffered(1))


def _prompt_layer(x, ng, win, wpool, pscale, wa, wb, wout, fg):
    b, t, d = x.shape
    nt = t // TQ
    tile = pl.BlockSpec((1, TQ, d), lambda i, j: (i, j, 0))
    kv_tile = pl.BlockSpec((1, TQ * HEADS, HEAD_DIM), lambda i, j: (i, j, 0))
    kv_shape = jax.ShapeDtypeStruct((b, t * HEADS, HEAD_DIM), F32)
    out_shape = (
        jax.ShapeDtypeStruct((b, t, d), F32),
        kv_shape,
        kv_shape,
        jax.ShapeDtypeStruct((b, POOL_HIST, d), F32),
    )
    return pl.pallas_call(
        _prompt_kernel,
        grid=(b, nt),
        in_specs=[tile, _const_spec(ng.shape), _const_spec(win.shape), _const_spec(wpool.shape),
                  _const_spec(pscale.shape), _const_spec(wa.shape), _const_spec(wb.shape),
                  _const_spec(wout.shape), _const_spec(fg.shape)],
        out_specs=(tile, kv_tile, kv_tile, pl.BlockSpec((1, POOL_HIST, d), lambda i, j: (i, 0, 0))),
        out_shape=out_shape,
        scratch_shapes=[
            pltpu.VMEM((HEADS, t, HEAD_DIM), BF16),
            pltpu.VMEM((HEADS, t, HEAD_DIM), BF16),
            pltpu.VMEM((HEADS, TQ, HEAD_DIM), BF16),
            pltpu.VMEM((TQ, d), BF16),
            pltpu.VMEM((POOL_HIST + TQ, d), F32),
            pltpu.VMEM((TQ, d), BF16),
            pltpu.VMEM((TQ, 2 * d), F32),
            pltpu.VMEM((TQ, d), BF16),
            pltpu.VMEM((HEADS, TQ, HEAD_DIM), F32),
            pltpu.VMEM((HEADS, TQ, HEAD_DIM), F32),
            pltpu.VMEM((TQ, TQ), BF16),
            pltpu.VMEM((TQ, d), F32),
        ],
        compiler_params=pltpu.CompilerParams(
            dimension_semantics=("arbitrary", "arbitrary"), vmem_limit_bytes=VMEM_LIMIT),
        name="prompt_layer",
    )(x, ng, win, wpool, pscale, wa, wb, wout, fg)


def _sample_proj_kernel(x_ref, ng_ref, win_ref, z_ref):
    h = _rms(x_ref[...], ng_ref[...]).astype(BF16)
    z_ref[0] = jnp.dot(h, win_ref[...], preferred_element_type=F32)


def _sample_proj(x2, ng, win):
    n, d = x2.shape
    nc = win.shape[1] // d
    return pl.pallas_call(
        _sample_proj_kernel,
        grid=(nc,),
        in_specs=[pl.BlockSpec((n, d), lambda c: (0, 0)), pl.BlockSpec(ng.shape, lambda c: (0, 0)),
                  pl.BlockSpec((d, d), lambda c: (0, c))],
        out_specs=pl.BlockSpec((1, n, d), lambda c: (c, 0, 0)),
        out_shape=jax.ShapeDtypeStruct((nc, n, d), F32),
        compiler_params=pltpu.CompilerParams(dimension_semantics=("arbitrary",)),
        name="sample_proj",
    )(x2, ng, win)


def _sample_kernel(x_ref, z_ref, hist_ref, wpool_ref, pscale_ref, wa_ref, wb_ref, wout_ref, fg_ref,
                   ck_hbm, cv_hbm, y_ref, pool_ref,
                   acc_ref, used_ref, uext_ref, tri_ref, kbuf, vbuf, sem, *, past_len):
    b = pl.program_id(0)
    ts = x_ref.shape[1]
    nkb = past_len // TKC
    blk_rows = TKC * HEADS

    def cache_copies(c, slot):
        rows = pl.ds(pl.multiple_of(c * blk_rows, blk_rows), blk_rows)
        return (pltpu.make_async_copy(ck_hbm.at[b, rows], kbuf.at[slot], sem.at[0, slot]),
                pltpu.make_async_copy(cv_hbm.at[b, rows], vbuf.at[slot], sem.at[1, slot]))

    def start(c, slot):
        for cp in cache_copies(c, slot):
            cp.start()

    def wait(c, slot):
        for cp in cache_copies(c, slot):
            cp.wait()

    start(nkb - 1, 0)

    heads = range(HEADS)

    def chunk_heads(c, scale=None):
        z = z_ref[c] if scale is None else z_ref[c] * scale
        z = z.astype(BF16)
        return [z[:, hd * HEAD_DIM:(hd + 1) * HEAD_DIM] for hd in heads]

    qs = chunk_heads(0, SCALE)
    _sb_block(qs, chunk_heads(1), chunk_heads(2), _tri(ts), acc_ref, used_ref, True, 0.0)

    tri_ref[...] = _tri(TKC)

    def more(c):
        i, least_used = c
        return jnp.logical_and(i < nkb, least_used < USED_CAP)

    def body(c):
        i, _ = c
        slot = lax.rem(i, 2)
        wait(nkb - 1 - i, slot)

        @pl.when(i + 1 < nkb)
        def _():
            start(nkb - 2 - i, 1 - slot)

        _sb_block(chunk_heads(0, SCALE),
                  [_head_rows(kbuf, slot, 0, TKC, hd).astype(BF16) for hd in heads],
                  [_head_rows(vbuf, slot, 0, TKC, hd).astype(BF16) for hd in heads],
                  tri_ref[...], acc_ref, used_ref, False, None)
        return i + 1, _least_used(used_ref)

    done, _ = lax.while_loop(more, body, (jnp.int32(0), _least_used(used_ref)))

    @pl.when(done < nkb)
    def _():
        wait(nkb - 1 - done, lax.rem(done, 2))

    att = jnp.concatenate([acc_ref[hd] for hd in range(HEADS)], axis=-1)
    att_g = (att * _silu(z_ref[3])).astype(BF16)
    uext_ref[0:POOL_HIST, :] = hist_ref[0]
    uext_ref[POOL_HIST:POOL_HIST + ts, :] = z_ref[4]
    pool_ref[0] = uext_ref[ts:ts + POOL_HIST, :]
    pb = _pool(uext_ref, ts, past_len, wpool_ref, pscale_ref)
    pb_g = (pb * _silu(z_ref[5])).astype(BF16)
    y_b = jnp.dot(pb_g, wb_ref[...], preferred_element_type=F32)
    y_ref[0] = _merge(x_ref[0], att_g, y_b, jax.nn.sigmoid(z_ref[6]), jax.nn.sigmoid(z_ref[7]),
                      wa_ref, wout_ref, fg_ref[...])


def _sample_layer(x, z, ck, cv, hist, wpool, pscale, wa, wb, wout, fg):
    b, ts, d = x.shape
    past_len = ck.shape[1] // HEADS
    nc = z.shape[0]
    hbm = pl.BlockSpec(memory_space=pl.ANY)
    return pl.pallas_call(
        functools.partial(_sample_kernel, past_len=past_len),
        grid=(b,),
        in_specs=[pl.BlockSpec((1, ts, d), lambda i: (i, 0, 0)),
                  pl.BlockSpec((nc, ts, d), lambda i: (0, i, 0)),
                  pl.BlockSpec((1, POOL_HIST, d), lambda i: (i, 0, 0)),
                  _const_spec(wpool.shape), _const_spec(pscale.shape), _const_spec(wa.shape),
                  _const_spec(wb.shape), _const_spec(wout.shape), _const_spec(fg.shape),
                  hbm, hbm],
        out_specs=(pl.BlockSpec((1, ts, d), lambda i: (i, 0, 0)),
                   pl.BlockSpec((1, POOL_HIST, d), lambda i: (i, 0, 0))),
        out_shape=(jax.ShapeDtypeStruct((b, ts, d), F32),
                   jax.ShapeDtypeStruct((b, POOL_HIST, d), F32)),
        scratch_shapes=[
            pltpu.VMEM((HEADS, ts, HEAD_DIM), F32),
            pltpu.VMEM((HEADS, ts, HEAD_DIM), F32),
            pltpu.VMEM((POOL_HIST + ts, d), F32),
            pltpu.VMEM((TKC, TKC), BF16),
            pltpu.VMEM((2, TKC * HEADS, HEAD_DIM), F32),
            pltpu.VMEM((2, TKC * HEADS, HEAD_DIM), F32),
            pltpu.SemaphoreType.DMA((2, 2)),
        ],
        compiler_params=pltpu.CompilerParams(
            dimension_semantics=("arbitrary",), vmem_limit_bytes=VMEM_LIMIT),
        name="sample_layer",
    )(x, z, hist, wpool, pscale, wa, wb, wout, fg, ck, cv)


def kernel(x_prompt, x_sample, cache_k, cache_v, state_pool, norm_g, w_in, w_pool, pool_scale,
           w_br_a, w_br_b, w_out, final_g):
    depth = norm_g.shape[0]
    assert depth == 1, "single-layer stack"
    b, t, d = x_prompt.shape
    bs, ts, _ = x_sample.shape
    past_len = cache_k.shape[2]
    assert t % TQ == 0 and past_len % TKC == 0
    assert cache_k.shape[3:] == (HEADS, HEAD_DIM) and d == HEADS * HEAD_DIM

    ng = norm_g[0].reshape(1, d)
    fg = final_g.reshape(1, d)
    pscale = pool_scale[0].reshape(1, d)
    win = w_in[0].astype(BF16)
    wpool = w_pool[0].astype(BF16)
    wa = w_br_a[0].astype(BF16)
    wb = w_br_b[0].astype(BF16)
    wout = w_out[0].astype(BF16)

    y_p, k_p, v_p, pool_p = _prompt_layer(x_prompt, ng, win, wpool, pscale, wa, wb, wout, fg)

    z = _sample_proj(x_sample.reshape(bs * ts, d), ng, win)
    hist = jnp.pad(state_pool[0], ((0, 0), (POOL_HIST - state_pool.shape[2], 0), (0, 0)))
    y_s, pool_s = _sample_layer(x_sample, z, cache_k.reshape(bs, past_len * HEADS, HEAD_DIM),
                                cache_v.reshape(bs, past_len * HEADS, HEAD_DIM), hist, wpool, pscale,
                                wa, wb, wout, fg)

    nbuf = state_pool.shape[2]
    hs = (HEADS, HEAD_DIM)
    return (y_p, y_s,
            k_p.reshape(1, b, t, *hs), v_p.reshape(1, b, t, *hs), pool_p[None, :, POOL_HIST - nbuf:],
            z[1].reshape(1, bs, ts, *hs), z[2].reshape(1, bs, ts, *hs), pool_s[None, :, POOL_HIST - nbuf:])
```

```python
import functools

import jax
import jax.numpy as jnp
from jax import lax
from jax.experimental import pallas as pl
from jax.experimental.pallas import tpu as pltpu

F32 = jnp.float32
BF16 = jnp.bfloat16

HEADS = 8
HEAD_DIM = 128
POOL_WINDOWS = (2, 4, 8, 16)
POOL_HIST = 16
EPS = 1e-6
SCALE = HEAD_DIM ** -0.5
LOG2E = 1.4426950408889634
USED_CAP = 88.0

TQ = 256
TKC = 256
HEAD_GROUP = 4
VMEM_LIMIT = 56 * 1024 * 1024


def _rms(x, g):
    return x * lax.rsqrt(jnp.mean(x * x, axis=-1, keepdims=True) + EPS) * g


def _sigmoid(x):
    return 0.5 * jnp.tanh(0.5 * x) + 0.5


def _silu(x):
    return x * _sigmoid(x)


def _tri(n):
    j = lax.broadcasted_iota(jnp.int32, (n, n), 0)
    s = lax.broadcasted_iota(jnp.int32, (n, n), 1)
    return (j > s).astype(BF16)


def _visible(tq, tk, lead):
    r = lax.broadcasted_iota(jnp.int32, (tq, tk), 0)
    c = lax.broadcasted_iota(jnp.int32, (tq, tk), 1)
    return c < r + lead


def _head_rows(ref, lead, row0, rows, hd):
    return ref[lead, pl.ds(row0 * HEADS + hd, rows, stride=HEADS), :]


def _sb_block(qs, ks, vs, tri, acc_ref, used_ref, lead, used0, fillers=None):
    first = used0 is not None
    def fill():
        thunk = next(fillers, None) if fillers is not None else None
        if thunk is not None:
            thunk()

    if len(qs) > HEAD_GROUP:
        for g in range(0, len(qs), HEAD_GROUP):
            grp = slice(g, g + HEAD_GROUP)
            _sb_block(qs[grp], ks[grp], vs[grp], tri, acc_ref.at[grp], used_ref.at[grp], lead, used0,
                      fillers)
        return
    nh = len(qs)
    tq, tk = qs[0].shape[0], ks[0].shape[0]
    mask = None if lead is None else _visible(tq, tk, lead)
    stage1 = []
    for hd in range(nh):
        s = lax.dot_general(qs[hd], ks[hd], (((1,), (1,)), ((), ())), preferred_element_type=F32)
        l1p = jnp.log(1.0 + jnp.exp2(jnp.abs(s) * (-LOG2E)))
        neg_log_1m = jnp.maximum(s, 0.0) + l1p
        log_beta = s - neg_log_1m
        if mask is not None:
            neg_log_1m = jnp.where(mask, neg_log_1m, 0.0)
        tot = jnp.broadcast_to(jnp.sum(neg_log_1m, axis=-1, keepdims=True), acc_ref.shape[1:])
        stage1.append((neg_log_1m.astype(BF16), log_beta, tot))
    fill()
    stage2 = []
    for hd in range(nh):
        neg_log_1m, log_beta, tot = stage1[hd]
        later = jnp.dot(neg_log_1m, tri, preferred_element_type=F32)
        if first:
            used_ref[hd] = tot + used0
        else:
            used = used_ref[hd]
            later = later + jnp.concatenate([used] * (tk // HEAD_DIM), axis=-1)
            used_ref[hd] = used + tot
        a = jnp.exp(log_beta - later)
        if mask is not None:
            a = jnp.where(mask, a, 0.0)
        stage2.append(a.astype(BF16))
    fill()
    for hd in range(nh):
        pv = jnp.dot(stage2[hd], vs[hd], preferred_element_type=F32)
        if first:
            acc_ref[hd] = pv
        else:
            acc_ref[hd] += pv
    fill()


def _least_used(used_ref):
    m = used_ref[0]
    for hd in range(1, HEADS):
        m = jnp.minimum(m, used_ref[hd])
    return jnp.min(m)


def _pool(uext_ref, rows, pos0, wpool_ref, pscale_ref):
    n = POOL_HIST + rows
    gdim = uext_ref.shape[1] // len(POOL_WINDOWS)
    pos = pos0 + lax.broadcasted_iota(jnp.int32, (rows, 1), 0)
    outs = []
    for g, w in enumerate(POOL_WINDOWS):
        cols = slice(g * gdim, (g + 1) * gdim)
        e = uext_ref[:, cols]
        acc = e
        span = 1
        while span < w:
            acc = acc + pltpu.roll(acc, span, axis=0)
            span *= 2
        inv = 1.0 / jnp.minimum(pos + 1, w).astype(F32)
        p = acc[POOL_HIST:n] * inv - e[POOL_HIST:n]
        o = jnp.dot(p.astype(BF16), wpool_ref[g], preferred_element_type=F32)
        outs.append(o)
    return jnp.concatenate(outs, axis=-1) * pscale_ref[...]


def _merge(x, att_g, y_b, gate_a, gate_b, wa_ref, wout_ref, fg):
    y_a = jnp.dot(att_g, wa_ref[...], preferred_element_type=F32)
    m = gate_a * y_a + gate_b * y_b
    out = x + jnp.dot(m.astype(BF16), wout_ref[...], preferred_element_type=F32)
    return _rms(out, fg)


def _prompt_kernel(x_ref, ng_ref, win_ref, wpool_ref, pscale_ref, wa_ref, wb_ref, wout_ref, fg_ref,
                   y_ref, k_ref, v_ref, pool_ref,
                   kh_ref, vh_ref, q_ref, sga_ref, uext_ref, sgb_ref, gate_ref, attg_ref,
                   acc_ref, used_ref, tri_ref, yb_ref):
    t = pl.program_id(1)
    nt = pl.num_programs(1)
    d = x_ref.shape[2]
    row0 = pl.multiple_of(t * TQ, TQ)

    @pl.when(t == 0)
    def _():
        tri_ref[...] = _tri(TQ)
        uext_ref[0:POOL_HIST, :] = jnp.zeros((POOL_HIST, d), F32)

    @pl.when(t > 0)
    def _():
        uext_ref[0:POOL_HIST, :] = uext_ref[TQ:TQ + POOL_HIST, :]

    x = x_ref[0]
    h = _rms(x, ng_ref[...]).astype(BF16)

    def proj(c):
        return jnp.dot(h, win_ref[:, c * d:(c + 1) * d], preferred_element_type=F32)

    def head(z, hd):
        return z[:, hd * HEAD_DIM:(hd + 1) * HEAD_DIM]

    q = (proj(0) * SCALE).astype(BF16)
    for hd in range(HEADS):
        q_ref[hd] = head(q, hd)
    k = proj(1)
    kb = k.astype(BF16)
    for hd in range(HEADS):
        k_ref[0, pl.ds(hd, TQ, stride=HEADS), :] = head(k, hd)
        kh_ref[hd, pl.ds(row0, TQ), :] = head(kb, hd)
    v = proj(2)
    vb = v.astype(BF16)
    for hd in range(HEADS):
        v_ref[0, pl.ds(hd, TQ, stride=HEADS), :] = head(v, hd)
        vh_ref[hd, pl.ds(row0, TQ), :] = head(vb, hd)
    def proj_ga():
        sga_ref[...] = _silu(proj(3)).astype(BF16)

    def proj_u():
        uext_ref[POOL_HIST:POOL_HIST + TQ, :] = proj(4)

    def proj_gb():
        sgb_ref[...] = _silu(proj(5)).astype(BF16)

    def proj_gate_a():
        gate_ref[:, 0:d] = _sigmoid(proj(6))

    def proj_gate_b():
        gate_ref[:, d:2 * d] = _sigmoid(proj(7))

    def branch_b():
        pb = _pool(uext_ref, TQ, t * TQ, wpool_ref, pscale_ref)
        pb_g = (pb * sgb_ref[...].astype(F32)).astype(BF16)
        yb_ref[...] = jnp.dot(pb_g, wb_ref[...], preferred_element_type=F32)

    heads = range(HEADS)
    half = TQ // 2
    top, bottom = slice(0, half), slice(half, TQ)
    fillers = iter((proj_ga, proj_u, proj_gb, proj_gate_a, proj_gate_b, branch_b))

    _sb_block([head(q, hd)[bottom] for hd in heads], [head(kb, hd) for hd in heads],
              [head(vb, hd) for hd in heads], tri_ref[...], acc_ref.at[:, bottom],
              used_ref.at[:, bottom], half, 0.0, fillers)
    lead_top = jnp.where(t > 0, half, 0)
    r_prev = pl.multiple_of(jnp.maximum(row0 - half, 0), half)

    def top_window(xh_ref, xb):
        return [jnp.concatenate([xh_ref[hd, pl.ds(r_prev, half), :], head(xb, hd)[top]], axis=0)
                for hd in heads]

    _sb_block([head(q, hd)[top] for hd in heads], top_window(kh_ref, kb), top_window(vh_ref, vb),
              tri_ref[...], acc_ref.at[:, top], used_ref.at[:, top], lead_top, 0.0, fillers)
    for thunk in fillers:
        if thunk is not None:
            thunk()

    def earlier_keys_while_needed(rows, n_chunks, chunk_start, nk):
        def more(c):
            i, least_used = c
            return jnp.logical_and(i < n_chunks, least_used < USED_CAP)

        def body(c):
            i, _ = c
            r = pl.multiple_of(chunk_start(i), nk)
            _sb_block([q_ref[hd, rows, :] for hd in heads],
                      [kh_ref[hd, pl.ds(r, nk), :] for hd in heads],
                      [vh_ref[hd, pl.ds(r, nk), :] for hd in heads], tri_ref[0:nk, 0:nk],
                      acc_ref.at[:, rows], used_ref.at[:, rows], None, None)
            return i + 1, _least_used(used_ref.at[:, rows])

        lax.while_loop(more, body, (jnp.int32(0), _least_used(used_ref.at[:, rows])))

    earlier_keys_while_needed(bottom, t, lambda i: (t - 1 - i) * TQ, TQ)
    earlier_keys_while_needed(top, 2 * t - 1, lambda i: row0 - half * (i + 2), half)
    for hd in range(HEADS):
        cols = slice(hd * HEAD_DIM, (hd + 1) * HEAD_DIM)
        attg_ref[:, cols] = (acc_ref[hd] * sga_ref[:, cols].astype(F32)).astype(BF16)

    y_ref[0] = _merge(x_ref[0], attg_ref[...], yb_ref[...], gate_ref[:, 0:d], gate_ref[:, d:2 * d],
                      wa_ref, wout_ref, fg_ref[...])

    @pl.when(t == nt - 1)
    def _():
        pool_ref[0] = uext_ref[TQ:TQ + POOL_HIST, :]


def _const_spec(shape):
    nd = len(shape)
    return pl.BlockSpec(shape, lambda *_: (0,) * nd, pipeline_mode=pl.Buffered(1))


def _prompt_layer(x, ng, win, wpool, pscale, wa, wb, wout, fg):
    b, t, d = x.shape
    nt = t // TQ
    tile = pl.BlockSpec((1, TQ, d), lambda i, j: (i, j, 0))
    kv_tile = pl.BlockSpec((1, TQ * HEADS, HEAD_DIM), lambda i, j: (i, j, 0))
    kv_shape = jax.ShapeDtypeStruct((b, t * HEADS, HEAD_DIM), F32)
    out_shape = (
        jax.ShapeDtypeStruct((b, t, d), F32),
        kv_shape,
        kv_shape,
        jax.ShapeDtypeStruct((b, POOL_HIST, d), F32),
    )
    return pl.pallas_call(
        _prompt_kernel,
        grid=(b, nt),
        in_specs=[tile, _const_spec(ng.shape), _const_spec(win.shape), _const_spec(wpool.shape),
                  _const_spec(pscale.shape), _const_spec(wa.shape), _const_spec(wb.shape),
                  _const_spec(wout.shape), _const_spec(fg.shape)],
        out_specs=(tile, kv_tile, kv_tile, pl.BlockSpec((1, POOL_HIST, d), lambda i, j: (i, 0, 0))),
        out_shape=out_shape,
        scratch_shapes=[
            pltpu.VMEM((HEADS, t, HEAD_DIM), BF16),
            pltpu.VMEM((HEADS, t, HEAD_DIM), BF16),
            pltpu.VMEM((HEADS, TQ, HEAD_DIM), BF16),
            pltpu.VMEM((TQ, d), BF16),
            pltpu.VMEM((POOL_HIST + TQ, d), F32),
            pltpu.VMEM((TQ, d), BF16),
            pltpu.VMEM((TQ, 2 * d), F32),
            pltpu.VMEM((TQ, d), BF16),
            pltpu.VMEM((HEADS, TQ, HEAD_DIM), F32),
            pltpu.VMEM((HEADS, TQ, HEAD_DIM), F32),
            pltpu.VMEM((TQ, TQ), BF16),
            pltpu.VMEM((TQ, d), F32),
        ],
        compiler_params=pltpu.CompilerParams(
            dimension_semantics=("arbitrary", "arbitrary"), vmem_limit_bytes=VMEM_LIMIT),
        name="prompt_layer",
    )(x, ng, win, wpool, pscale, wa, wb, wout, fg)


def _sample_proj_kernel(x_ref, ng_ref, win_ref, z_ref):
    h = _rms(x_ref[...], ng_ref[...]).astype(BF16)
    z_ref[0] = jnp.dot(h, win_ref[...], preferred_element_type=F32)


def _sample_proj(x2, ng, win):
    n, d = x2.shape
    nc = win.shape[1] // d
    return pl.pallas_call(
        _sample_proj_kernel,
        grid=(nc,),
        in_specs=[pl.BlockSpec((n, d), lambda c: (0, 0)), pl.BlockSpec(ng.shape, lambda c: (0, 0)),
                  pl.BlockSpec((d, d), lambda c: (0, c))],
        out_specs=pl.BlockSpec((1, n, d), lambda c: (c, 0, 0)),
        out_shape=jax.ShapeDtypeStruct((nc, n, d), F32),
        compiler_params=pltpu.CompilerParams(dimension_semantics=("arbitrary",)),
        name="sample_proj",
    )(x2, ng, win)


def _sample_kernel(x_ref, z_ref, hist_ref, wpool_ref, pscale_ref, wa_ref, wb_ref, wout_ref, fg_ref,
                   ck_hbm, cv_hbm, y_ref, pool_ref,
                   acc_ref, used_ref, uext_ref, tri_ref, kbuf, vbuf, sem, *, past_len):
    b = pl.program_id(0)
    ts = x_ref.shape[1]
    nkb = past_len // TKC
    blk_rows = TKC * HEADS

    def cache_copies(c, slot):
        rows = pl.ds(pl.multiple_of(c * blk_rows, blk_rows), blk_rows)
        return (pltpu.make_async_copy(ck_hbm.at[b, rows], kbuf.at[slot], sem.at[0, slot]),
                pltpu.make_async_copy(cv_hbm.at[b, rows], vbuf.at[slot], sem.at[1, slot]))

    def start(c, slot):
        for cp in cache_copies(c, slot):
            cp.start()

    def wait(c, slot):
        for cp in cache_copies(c, slot):
            cp.wait()

    start(nkb - 1, 0)

    heads = range(HEADS)

    def chunk_heads(c, scale=None):
        z = z_ref[c] if scale is None else z_ref[c] * scale
        z = z.astype(BF16)
        return [z[:, hd * HEAD_DIM:(hd + 1) * HEAD_DIM] for hd in heads]

    qs = chunk_heads(0, SCALE)
    _sb_block(qs, chunk_heads(1), chunk_heads(2), _tri(ts), acc_ref, used_ref, 0, 0.0)

    tri_ref[...] = _tri(TKC)

    def more(c):
        i, least_used = c
        return jnp.logical_and(i < nkb, least_used < USED_CAP)

    def body(c):
        i, _ = c
        slot = lax.rem(i, 2)
        wait(nkb - 1 - i, slot)

        @pl.when(i + 1 < nkb)
        def _():
            start(nkb - 2 - i, 1 - slot)

        _sb_block(chunk_heads(0, SCALE),
                  [_head_rows(kbuf, slot, 0, TKC, hd).astype(BF16) for hd in heads],
                  [_head_rows(vbuf, slot, 0, TKC, hd).astype(BF16) for hd in heads],
                  tri_ref[...], acc_ref, used_ref, None, None)
        return i + 1, _least_used(used_ref)

    done, _ = lax.while_loop(more, body, (jnp.int32(0), _least_used(used_ref)))

    @pl.when(done < nkb)
    def _():
        wait(nkb - 1 - done, lax.rem(done, 2))

    att = jnp.concatenate([acc_ref[hd] for hd in range(HEADS)], axis=-1)
    att_g = (att * _silu(z_ref[3])).astype(BF16)
    uext_ref[0:POOL_HIST, :] = hist_ref[0]
    uext_ref[POOL_HIST:POOL_HIST + ts, :] = z_ref[4]
    pool_ref[0] = uext_ref[ts:ts + POOL_HIST, :]
    pb = _pool(uext_ref, ts, past_len, wpool_ref, pscale_ref)
    pb_g = (pb * _silu(z_ref[5])).astype(BF16)
    y_b = jnp.dot(pb_g, wb_ref[...], preferred_element_type=F32)
    y_ref[0] = _merge(x_ref[0], att_g, y_b, _sigmoid(z_ref[6]), _sigmoid(z_ref[7]),
                      wa_ref, wout_ref, fg_ref[...])


def _sample_layer(x, z, ck, cv, hist, wpool, pscale, wa, wb, wout, fg):
    b, ts, d = x.shape
    past_len = ck.shape[1] // HEADS
    nc = z.shape[0]
    hbm = pl.BlockSpec(memory_space=pl.ANY)
    return pl.pallas_call(
        functools.partial(_sample_kernel, past_len=past_len),
        grid=(b,),
        in_specs=[pl.BlockSpec((1, ts, d), lambda i: (i, 0, 0)),
                  pl.BlockSpec((nc, ts, d), lambda i: (0, i, 0)),
                  pl.BlockSpec((1, POOL_HIST, d), lambda i: (i, 0, 0)),
                  _const_spec(wpool.shape), _const_spec(pscale.shape), _const_spec(wa.shape),
                  _const_spec(wb.shape), _const_spec(wout.shape), _const_spec(fg.shape),
                  hbm, hbm],
        out_specs=(pl.BlockSpec((1, ts, d), lambda i: (i, 0, 0)),
                   pl.BlockSpec((1, POOL_HIST, d), lambda i: (i, 0, 0))),
        out_shape=(jax.ShapeDtypeStruct((b, ts, d), F32),
                   jax.ShapeDtypeStruct((b, POOL_HIST, d), F32)),
        scratch_shapes=[
            pltpu.VMEM((HEADS, ts, HEAD_DIM), F32),
            pltpu.VMEM((HEADS, ts, HEAD_DIM), F32),
            pltpu.VMEM((POOL_HIST + ts, d), F32),
            pltpu.VMEM((TKC, TKC), BF16),
            pltpu.VMEM((2, TKC * HEADS, HEAD_DIM), F32),
            pltpu.VMEM((2, TKC * HEADS, HEAD_DIM), F32),
            pltpu.SemaphoreType.DMA((2, 2)),
        ],
        compiler_params=pltpu.CompilerParams(
            dimension_semantics=("arbitrary",), vmem_limit_bytes=VMEM_LIMIT),
        name="sample_layer",
    )(x, z, hist, wpool, pscale, wa, wb, wout, fg, ck, cv)


def kernel(x_prompt, x_sample, cache_k, cache_v, state_pool, norm_g, w_in, w_pool, pool_scale,
           w_br_a, w_br_b, w_out, final_g):
    depth = norm_g.shape[0]
    assert depth == 1, "single-layer stack"
    b, t, d = x_prompt.shape
    bs, ts, _ = x_sample.shape
    past_len = cache_k.shape[2]
    assert t % TQ == 0 and past_len % TKC == 0
    assert cache_k.shape[3:] == (HEADS, HEAD_DIM) and d == HEADS * HEAD_DIM

    ng = norm_g[0].reshape(1, d)
    fg = final_g.reshape(1, d)
    pscale = pool_scale[0].reshape(1, d)
    win = w_in[0].astype(BF16)
    wpool = w_pool[0].astype(BF16)
    wa = w_br_a[0].astype(BF16)
    wb = w_br_b[0].astype(BF16)
    wout = w_out[0].astype(BF16)

    y_p, k_p, v_p, pool_p = _prompt_layer(x_prompt, ng, win, wpool, pscale, wa, wb, wout, fg)

    z = _sample_proj(x_sample.reshape(bs * ts, d), ng, win)
    hist = jnp.pad(state_pool[0], ((0, 0), (POOL_HIST - state_pool.shape[2], 0), (0, 0)))
    y_s, pool_s = _sample_layer(x_sample, z, cache_k.reshape(bs, past_len * HEADS, HEAD_DIM),
                                cache_v.reshape(bs, past_len * HEADS, HEAD_DIM), hist, wpool, pscale,
                                wa, wb, wout, fg)

    nbuf = state_pool.shape[2]
    hs = (HEADS, HEAD_DIM)
    return (y_p, y_s,
            k_p.reshape(1, b, t, *hs), v_p.reshape(1, b, t, *hs), pool_p[None, :, POOL_HIST - nbuf:],
            z[1].reshape(1, bs, ts, *hs), z[2].reshape(1, bs, ts, *hs), pool_s[None, :, POOL_HIST - nbuf:])
```

```python
import functools

import jax
import jax.numpy as jnp
from jax import lax
from jax.experimental import pallas as pl
from jax.experimental.pallas import tpu as pltpu

F32 = jnp.float32
BF16 = jnp.bfloat16

HEADS = 8
HEAD_DIM = 128
POOL_WINDOWS = (2, 4, 8, 16)
POOL_HIST = 16
EPS = 1e-6
SCALE = HEAD_DIM ** -0.5
LOG2E = 1.4426950408889634
USED_CAP = 88.0

TQ = 256
TKC = 256
HEAD_GROUP = 8
VMEM_LIMIT = 56 * 1024 * 1024


def _rms(x, g):
    return x * lax.rsqrt(jnp.mean(x * x, axis=-1, keepdims=True) + EPS) * g


def _sigmoid(x):
    return 0.5 * jnp.tanh(0.5 * x) + 0.5


def _silu(x):
    return x * _sigmoid(x)


def _tri(n):
    j = lax.broadcasted_iota(jnp.int32, (n, n), 0)
    s = lax.broadcasted_iota(jnp.int32, (n, n), 1)
    return (j > s).astype(BF16)


def _visible(tq, tk, lead):
    r = lax.broadcasted_iota(jnp.int32, (tq, tk), 0)
    c = lax.broadcasted_iota(jnp.int32, (tq, tk), 1)
    return c < r + lead


def _head_rows(ref, lead, row0, rows, hd):
    return ref[lead, pl.ds(row0 * HEADS + hd, rows, stride=HEADS), :]


def _sb_block(qs, ks, vs, tri, acc_ref, used_ref, lead, used0, fillers=None):
    first = used0 is not None
    def fill():
        thunk = next(fillers, None) if fillers is not None else None
        if thunk is not None:
            thunk()

    if len(qs) > HEAD_GROUP:
        for g in range(0, len(qs), HEAD_GROUP):
            grp = slice(g, g + HEAD_GROUP)
            _sb_block(qs[grp], ks[grp], vs[grp], tri, acc_ref.at[grp], used_ref.at[grp], lead, used0,
                      fillers)
        return
    nh = len(qs)
    tq, tk = qs[0].shape[0], ks[0].shape[0]
    mask = None if lead is None else _visible(tq, tk, lead)
    stage1 = []
    for hd in range(nh):
        s = lax.dot_general(qs[hd], ks[hd], (((1,), (1,)), ((), ())), preferred_element_type=F32)
        l1p = jnp.log(1.0 + jnp.exp2(jnp.abs(s) * (-LOG2E)))
        neg_log_1m = jnp.maximum(s, 0.0) + l1p
        log_beta = s - neg_log_1m
        if mask is not None:
            neg_log_1m = jnp.where(mask, neg_log_1m, 0.0)
        tot = jnp.broadcast_to(jnp.sum(neg_log_1m, axis=-1, keepdims=True), acc_ref.shape[1:])
        stage1.append((neg_log_1m.astype(BF16), log_beta, tot))
    fill()
    stage2 = []
    for hd in range(nh):
        neg_log_1m, log_beta, tot = stage1[hd]
        later = jnp.dot(neg_log_1m, tri, preferred_element_type=F32)
        if first:
            used_ref[hd] = tot + used0
        else:
            used = used_ref[hd]
            later = later + jnp.concatenate([used] * (tk // HEAD_DIM), axis=-1)
            used_ref[hd] = used + tot
        a = jnp.exp(log_beta - later)
        if mask is not None:
            a = jnp.where(mask, a, 0.0)
        stage2.append(a.astype(BF16))
    fill()
    for hd in range(nh):
        pv = jnp.dot(stage2[hd], vs[hd], preferred_element_type=F32)
        if first:
            acc_ref[hd] = pv
        else:
            acc_ref[hd] += pv
    fill()


def _least_used(used_ref):
    m = used_ref[0]
    for hd in range(1, HEADS):
        m = jnp.minimum(m, used_ref[hd])
    return jnp.min(m)


def _pool(uext_ref, rows, pos0, wpool_ref, pscale_ref):
    n = POOL_HIST + rows
    gdim = uext_ref.shape[1] // len(POOL_WINDOWS)
    pos = pos0 + lax.broadcasted_iota(jnp.int32, (rows, 1), 0)
    outs = []
    for g, w in enumerate(POOL_WINDOWS):
        cols = slice(g * gdim, (g + 1) * gdim)
        e = uext_ref[:, cols]
        acc = e
        span = 1
        while span < w:
            acc = acc + pltpu.roll(acc, span, axis=0)
            span *= 2
        inv = 1.0 / jnp.minimum(pos + 1, w).astype(F32)
        p = acc[POOL_HIST:n] * inv - e[POOL_HIST:n]
        o = jnp.dot(p.astype(BF16), wpool_ref[g], preferred_element_type=F32)
        outs.append(o)
    return jnp.concatenate(outs, axis=-1) * pscale_ref[...]


def _merge(x, att_g, y_b, gate_a, gate_b, wa_ref, wout_ref, fg):
    y_a = jnp.dot(att_g, wa_ref[...], preferred_element_type=F32)
    m = gate_a * y_a + gate_b * y_b
    out = x + jnp.dot(m.astype(BF16), wout_ref[...], preferred_element_type=F32)
    return _rms(out, fg)


def _prompt_kernel(x_ref, ng_ref, win_ref, wpool_ref, pscale_ref, wa_ref, wb_ref, wout_ref, fg_ref,
                   y_ref, k_ref, v_ref, pool_ref,
                   kh_ref, vh_ref, q_ref, sga_ref, uext_ref, sgb_ref, gate_ref, attg_ref,
                   acc_ref, used_ref, tri_ref, yb_ref):
    t = pl.program_id(1)
    nt = pl.num_programs(1)
    d = x_ref.shape[2]
    row0 = pl.multiple_of(t * TQ, TQ)

    @pl.when(t == 0)
    def _():
        tri_ref[...] = _tri(TQ)
        uext_ref[0:POOL_HIST, :] = jnp.zeros((POOL_HIST, d), F32)

    @pl.when(t > 0)
    def _():
        uext_ref[0:POOL_HIST, :] = uext_ref[TQ:TQ + POOL_HIST, :]

    x = x_ref[0]
    h = _rms(x, ng_ref[...]).astype(BF16)

    def proj(c):
        return jnp.dot(h, win_ref[:, c * d:(c + 1) * d], preferred_element_type=F32)

    def head(z, hd):
        return z[:, hd * HEAD_DIM:(hd + 1) * HEAD_DIM]

    q = (proj(0) * SCALE).astype(BF16)
    for hd in range(HEADS):
        q_ref[hd] = head(q, hd)
    k = proj(1)
    kb = k.astype(BF16)
    for hd in range(HEADS):
        k_ref[0, pl.ds(hd, TQ, stride=HEADS), :] = head(k, hd)
        kh_ref[hd, pl.ds(row0, TQ), :] = head(kb, hd)
    v = proj(2)
    vb = v.astype(BF16)
    for hd in range(HEADS):
        v_ref[0, pl.ds(hd, TQ, stride=HEADS), :] = head(v, hd)
        vh_ref[hd, pl.ds(row0, TQ), :] = head(vb, hd)
    def proj_ga():
        sga_ref[...] = _silu(proj(3)).astype(BF16)

    def proj_u():
        uext_ref[POOL_HIST:POOL_HIST + TQ, :] = proj(4)

    def proj_gb():
        sgb_ref[...] = _silu(proj(5)).astype(BF16)

    def proj_gate_a():
        gate_ref[:, 0:d] = _sigmoid(proj(6))

    def proj_gate_b():
        gate_ref[:, d:2 * d] = _sigmoid(proj(7))

    def branch_b():
        pb = _pool(uext_ref, TQ, t * TQ, wpool_ref, pscale_ref)
        pb_g = (pb * sgb_ref[...].astype(F32)).astype(BF16)
        yb_ref[...] = jnp.dot(pb_g, wb_ref[...], preferred_element_type=F32)

    heads = range(HEADS)
    half = TQ // 2
    top, bottom = slice(0, half), slice(half, TQ)
    fillers = iter((proj_ga, proj_u, proj_gb, proj_gate_a, proj_gate_b, branch_b))

    _sb_block([head(q, hd)[bottom] for hd in heads], [head(kb, hd) for hd in heads],
              [head(vb, hd) for hd in heads], tri_ref[...], acc_ref.at[:, bottom],
              used_ref.at[:, bottom], half, 0.0, fillers)
    lead_top = jnp.where(t > 0, half, 0)
    r_prev = pl.multiple_of(jnp.maximum(row0 - half, 0), half)

    def top_window(xh_ref, xb):
        return [jnp.concatenate([xh_ref[hd, pl.ds(r_prev, half), :], head(xb, hd)[top]], axis=0)
                for hd in heads]

    _sb_block([head(q, hd)[top] for hd in heads], top_window(kh_ref, kb), top_window(vh_ref, vb),
              tri_ref[...], acc_ref.at[:, top], used_ref.at[:, top], lead_top, 0.0, fillers)
    for thunk in fillers:
        if thunk is not None:
            thunk()

    def earlier_keys_while_needed(rows, n_chunks, chunk_start, nk):
        def more(c):
            i, least_used = c
            return jnp.logical_and(i < n_chunks, least_used < USED_CAP)

        def body(c):
            i, _ = c
            r = pl.multiple_of(chunk_start(i), nk)
            _sb_block([q_ref[hd, rows, :] for hd in heads],
                      [kh_ref[hd, pl.ds(r, nk), :] for hd in heads],
                      [vh_ref[hd, pl.ds(r, nk), :] for hd in heads], tri_ref[0:nk, 0:nk],
                      acc_ref.at[:, rows], used_ref.at[:, rows], None, None)
            return i + 1, _least_used(used_ref.at[:, rows])

        lax.while_loop(more, body, (jnp.int32(0), _least_used(used_ref.at[:, rows])))

    earlier_keys_while_needed(bottom, t, lambda i: (t - 1 - i) * TQ, TQ)
    earlier_keys_while_needed(top, 2 * t - 1, lambda i: row0 - half * (i + 2), half)
    for hd in range(HEADS):
        cols = slice(hd * HEAD_DIM, (hd + 1) * HEAD_DIM)
        attg_ref[:, cols] = (acc_ref[hd] * sga_ref[:, cols].astype(F32)).astype(BF16)

    y_ref[0] = _merge(x_ref[0], attg_ref[...], yb_ref[...], gate_ref[:, 0:d], gate_ref[:, d:2 * d],
                      wa_ref, wout_ref, fg_ref[...])

    @pl.when(t == nt - 1)
    def _():
        pool_ref[0] = uext_ref[TQ:TQ + POOL_HIST, :]


def _const_spec(shape):
    nd = len(shape)
    return pl.BlockSpec(shape, lambda *_: (0,) * nd, pipeline_mode=pl.Buffered(1))


def _prompt_layer(x, ng, win, wpool, pscale, wa, wb, wout, fg):
    b, t, d = x.shape
    nt = t // TQ
    tile = pl.BlockSpec((1, TQ, d), lambda i, j: (i, j, 0))
    kv_tile = pl.BlockSpec((1, TQ * HEADS, HEAD_DIM), lambda i, j: (i, j, 0))
    kv_shape = jax.ShapeDtypeStruct((b, t * HEADS, HEAD_DIM), F32)
    out_shape = (
        jax.ShapeDtypeStruct((b, t, d), F32),
        kv_shape,
        kv_shape,
        jax.ShapeDtypeStruct((b, POOL_HIST, d), F32),
    )
    return pl.pallas_call(
        _prompt_kernel,
        grid=(b, nt),
        in_specs=[tile, _const_spec(ng.shape), _const_spec(win.shape), _const_spec(wpool.shape),
                  _const_spec(pscale.shape), _const_spec(wa.shape), _const_spec(wb.shape),
                  _const_spec(wout.shape), _const_spec(fg.shape)],
        out_specs=(tile, kv_tile, kv_tile, pl.BlockSpec((1, POOL_HIST, d), lambda i, j: (i, 0, 0))),
        out_shape=out_shape,
        scratch_shapes=[
            pltpu.VMEM((HEADS, t, HEAD_DIM), BF16),
            pltpu.VMEM((HEADS, t, HEAD_DIM), BF16),
            pltpu.VMEM((HEADS, TQ, HEAD_DIM), BF16),
            pltpu.VMEM((TQ, d), BF16),
            pltpu.VMEM((POOL_HIST + TQ, d), F32),
            pltpu.VMEM((TQ, d), BF16),
            pltpu.VMEM((TQ, 2 * d), F32),
            pltpu.VMEM((TQ, d), BF16),
            pltpu.VMEM((HEADS, TQ, HEAD_DIM), F32),
            pltpu.VMEM((HEADS, TQ, HEAD_DIM), F32),
            pltpu.VMEM((TQ, TQ), BF16),
            pltpu.VMEM((TQ, d), F32),
        ],
        compiler_params=pltpu.CompilerParams(
            dimension_semantics=("arbitrary", "arbitrary"), vmem_limit_bytes=VMEM_LIMIT),
        name="prompt_layer",
    )(x, ng, win, wpool, pscale, wa, wb, wout, fg)


def _sample_proj_kernel(x_ref, ng_ref, win_ref, z_ref):
    h = _rms(x_ref[...], ng_ref[...]).astype(BF16)
    z_ref[0] = jnp.dot(h, win_ref[...], preferred_element_type=F32)


def _sample_proj(x2, ng, win):
    n, d = x2.shape
    nc = win.shape[1] // d
    return pl.pallas_call(
        _sample_proj_kernel,
        grid=(nc,),
        in_specs=[pl.BlockSpec((n, d), lambda c: (0, 0)), pl.BlockSpec(ng.shape, lambda c: (0, 0)),
                  pl.BlockSpec((d, d), lambda c: (0, c))],
        out_specs=pl.BlockSpec((1, n, d), lambda c: (c, 0, 0)),
        out_shape=jax.ShapeDtypeStruct((nc, n, d), F32),
        compiler_params=pltpu.CompilerParams(dimension_semantics=("arbitrary",)),
        name="sample_proj",
    )(x2, ng, win)


def _sample_kernel(x_ref, z_ref, hist_ref, wpool_ref, pscale_ref, wa_ref, wb_ref, wout_ref, fg_ref,
                   ck_hbm, cv_hbm, y_ref, pool_ref,
                   acc_ref, used_ref, uext_ref, tri_ref, kbuf, vbuf, sem, *, past_len):
    b = pl.program_id(0)
    ts = x_ref.shape[1]
    nkb = past_len // TKC
    blk_rows = TKC * HEADS

    def cache_copies(c, slot):
        rows = pl.ds(pl.multiple_of(c * blk_rows, blk_rows), blk_rows)
        return (pltpu.make_async_copy(ck_hbm.at[b, rows], kbuf.at[slot], sem.at[0, slot]),
                pltpu.make_async_copy(cv_hbm.at[b, rows], vbuf.at[slot], sem.at[1, slot]))

    def start(c, slot):
        for cp in cache_copies(c, slot):
            cp.start()

    def wait(c, slot):
        for cp in cache_copies(c, slot):
            cp.wait()

    start(nkb - 1, 0)

    heads = range(HEADS)

    def chunk_heads(c, scale=None):
        z = z_ref[c] if scale is None else z_ref[c] * scale
        z = z.astype(BF16)
        return [z[:, hd * HEAD_DIM:(hd + 1) * HEAD_DIM] for hd in heads]

    qs = chunk_heads(0, SCALE)
    _sb_block(qs, chunk_heads(1), chunk_heads(2), _tri(ts), acc_ref, used_ref, 0, 0.0)

    tri_ref[...] = _tri(TKC)

    def more(c):
        i, least_used = c
        return jnp.logical_and(i < nkb, least_used < USED_CAP)

    def body(c):
        i, _ = c
        slot = lax.rem(i, 2)
        wait(nkb - 1 - i, slot)

        @pl.when(i + 1 < nkb)
        def _():
            start(nkb - 2 - i, 1 - slot)

        _sb_block(chunk_heads(0, SCALE),
                  [_head_rows(kbuf, slot, 0, TKC, hd).astype(BF16) for hd in heads],
                  [_head_rows(vbuf, slot, 0, TKC, hd).astype(BF16) for hd in heads],
                  tri_ref[...], acc_ref, used_ref, None, None)
        return i + 1, _least_used(used_ref)

    done, _ = lax.while_loop(more, body, (jnp.int32(0), _least_used(used_ref)))

    @pl.when(done < nkb)
    def _():
        wait(nkb - 1 - done, lax.rem(done, 2))

    att = jnp.concatenate([acc_ref[hd] for hd in range(HEADS)], axis=-1)
    att_g = (att * _silu(z_ref[3])).astype(BF16)
    uext_ref[0:POOL_HIST, :] = hist_ref[0]
    uext_ref[POOL_HIST:POOL_HIST + ts, :] = z_ref[4]
    pool_ref[0] = uext_ref[ts:ts + POOL_HIST, :]
    pb = _pool(uext_ref, ts, past_len, wpool_ref, pscale_ref)
    pb_g = (pb * _silu(z_ref[5])).astype(BF16)
    y_b = jnp.dot(pb_g, wb_ref[...], preferred_element_type=F32)
    y_ref[0] = _merge(x_ref[0], att_g, y_b, _sigmoid(z_ref[6]), _sigmoid(z_ref[7]),
                      wa_ref, wout_ref, fg_ref[...])


def _sample_layer(x, z, ck, cv, hist, wpool, pscale, wa, wb, wout, fg):
    b, ts, d = x.shape
    past_len = ck.shape[1] // HEADS
    nc = z.shape[0]
    hbm = pl.BlockSpec(memory_space=pl.ANY)
    return pl.pallas_call(
        functools.partial(_sample_kernel, past_len=past_len),
        grid=(b,),
        in_specs=[pl.BlockSpec((1, ts, d), lambda i: (i, 0, 0)),
                  pl.BlockSpec((nc, ts, d), lambda i: (0, i, 0)),
                  pl.BlockSpec((1, POOL_HIST, d), lambda i: (i, 0, 0)),
                  _const_spec(wpool.shape), _const_spec(pscale.shape), _const_spec(wa.shape),
                  _const_spec(wb.shape), _const_spec(wout.shape), _const_spec(fg.shape),
                  hbm, hbm],
        out_specs=(pl.BlockSpec((1, ts, d), lambda i: (i, 0, 0)),
                   pl.BlockSpec((1, POOL_HIST, d), lambda i: (i, 0, 0))),
        out_shape=(jax.ShapeDtypeStruct((b, ts, d), F32),
                   jax.ShapeDtypeStruct((b, POOL_HIST, d), F32)),
        scratch_shapes=[
            pltpu.VMEM((HEADS, ts, HEAD_DIM), F32),
            pltpu.VMEM((HEADS, ts, HEAD_DIM), F32),
            pltpu.VMEM((POOL_HIST + ts, d), F32),
            pltpu.VMEM((TKC, TKC), BF16),
            pltpu.VMEM((2, TKC * HEADS, HEAD_DIM), F32),
            pltpu.VMEM((2, TKC * HEADS, HEAD_DIM), F32),
            pltpu.SemaphoreType.DMA((2, 2)),
        ],
        compiler_params=pltpu.CompilerParams(
            dimension_semantics=("arbitrary",), vmem_limit_bytes=VMEM_LIMIT),
        name="sample_layer",
    )(x, z, hist, wpool, pscale, wa, wb, wout, fg, ck, cv)


def kernel(x_prompt, x_sample, cache_k, cache_v, state_pool, norm_g, w_in, w_pool, pool_scale,
           w_br_a, w_br_b, w_out, final_g):
    depth = norm_g.shape[0]
    assert depth == 1, "single-layer stack"
    b, t, d = x_prompt.shape
    bs, ts, _ = x_sample.shape
    past_len = cache_k.shape[2]
    assert t % TQ == 0 and past_len % TKC == 0
    assert cache_k.shape[3:] == (HEADS, HEAD_DIM) and d == HEADS * HEAD_DIM

    ng = norm_g[0].reshape(1, d)
    fg = final_g.reshape(1, d)
    pscale = pool_scale[0].reshape(1, d)
    win = w_in[0].astype(BF16)
    wpool = w_pool[0].astype(BF16)
    wa = w_br_a[0].astype(BF16)
    wb = w_br_b[0].astype(BF16)
    wout = w_out[0].astype(BF16)

    y_p, k_p, v_p, pool_p = _prompt_layer(x_prompt, ng, win, wpool, pscale, wa, wb, wout, fg)

    z = _sample_proj(x_sample.reshape(bs * ts, d), ng, win)
    hist = jnp.pad(state_pool[0], ((0, 0), (POOL_HIST - state_pool.shape[2], 0), (0, 0)))
    y_s, pool_s = _sample_layer(x_sample, z, cache_k.reshape(bs, past_len * HEADS, HEAD_DIM),
                                cache_v.reshape(bs, past_len * HEADS, HEAD_DIM), hist, wpool, pscale,
                                wa, wb, wout, fg)

    nbuf = state_pool.shape[2]
    hs = (HEADS, HEAD_DIM)
    return (y_p, y_s,
            k_p.reshape(1, b, t, *hs), v_p.reshape(1, b, t, *hs), pool_p[None, :, POOL_HIST - nbuf:],
            z[1].reshape(1, bs, ts, *hs), z[2].reshape(1, bs, ts, *hs), pool_s[None, :, POOL_HIST - nbuf:])
```

```python
import functools

import jax
import jax.numpy as jnp
from jax import lax
from jax.experimental import pallas as pl
from jax.experimental.pallas import tpu as pltpu

F32 = jnp.float32
BF16 = jnp.bfloat16

HEADS = 8
HEAD_DIM = 128
POOL_WINDOWS = (2, 4, 8, 16)
POOL_HIST = 16
EPS = 1e-6
SCALE = HEAD_DIM ** -0.5
LOG2E = 1.4426950408889634
USED_CAP = 88.0

TQ = 256
TKC = 256
HEAD_GROUP = 8
VMEM_LIMIT = 56 * 1024 * 1024


def _rms(x, g):
    return x * lax.rsqrt(jnp.mean(x * x, axis=-1, keepdims=True) + EPS) * g


def _sigmoid(x):
    return 0.5 * jnp.tanh(0.5 * x) + 0.5


def _silu(x):
    return x * _sigmoid(x)


def _tri(n):
    j = lax.broadcasted_iota(jnp.int32, (n, n), 0)
    s = lax.broadcasted_iota(jnp.int32, (n, n), 1)
    return (j > s).astype(BF16)


def _visible(tq, tk, lead):
    r = lax.broadcasted_iota(jnp.int32, (tq, tk), 0)
    c = lax.broadcasted_iota(jnp.int32, (tq, tk), 1)
    return c < r + lead


def _head_rows(ref, lead, row0, rows, hd):
    return ref[lead, pl.ds(row0 * HEADS + hd, rows, stride=HEADS), :]


def _sb_block(qs, ks, vs, tri, acc_ref, used_ref, lead, used0, fillers=None):
    first = used0 is not None
    def fill():
        thunk = next(fillers, None) if fillers is not None else None
        if thunk is not None:
            thunk()

    if len(qs) > HEAD_GROUP:
        for g in range(0, len(qs), HEAD_GROUP):
            grp = slice(g, g + HEAD_GROUP)
            _sb_block(qs[grp], ks[grp], vs[grp], tri, acc_ref.at[grp], used_ref.at[grp], lead, used0,
                      fillers)
        return
    nh = len(qs)
    tq, tk = qs[0].shape[0], ks[0].shape[0]
    mask = None if lead is None else _visible(tq, tk, lead)
    stage1 = []
    for hd in range(nh):
        s = lax.dot_general(qs[hd], ks[hd], (((1,), (1,)), ((), ())), preferred_element_type=F32)
        l1p = jnp.log(1.0 + jnp.exp2(jnp.abs(s) * (-LOG2E)))
        neg_log_1m = jnp.maximum(s, 0.0) + l1p
        log_beta = s - neg_log_1m
        if mask is not None:
            neg_log_1m = jnp.where(mask, neg_log_1m, 0.0)
        tot = jnp.broadcast_to(jnp.sum(neg_log_1m, axis=-1, keepdims=True), acc_ref.shape[1:])
        stage1.append((neg_log_1m.astype(BF16), log_beta, tot))
    fill()
    stage2 = []
    for hd in range(nh):
        neg_log_1m, log_beta, tot = stage1[hd]
        later = jnp.dot(neg_log_1m, tri, preferred_element_type=F32)
        if first:
            used_ref[hd] = tot + used0
        else:
            used = used_ref[hd]
            later = later + jnp.concatenate([used] * (tk // HEAD_DIM), axis=-1)
            used_ref[hd] = used + tot
        a = jnp.exp(log_beta - later)
        if mask is not None:
            a = jnp.where(mask, a, 0.0)
        stage2.append(a.astype(BF16))
    fill()
    for hd in range(nh):
        pv = jnp.dot(stage2[hd], vs[hd], preferred_element_type=F32)
        if first:
            acc_ref[hd] = pv
        else:
            acc_ref[hd] += pv
    fill()


def _least_used(used_ref):
    m = used_ref[0]
    for hd in range(1, HEADS):
        m = jnp.minimum(m, used_ref[hd])
    return jnp.min(m)


def _pool(uext_ref, rows, pos0, wpool_ref, pscale_ref):
    n = POOL_HIST + rows
    gdim = uext_ref.shape[1] // len(POOL_WINDOWS)
    pos = pos0 + lax.broadcasted_iota(jnp.int32, (rows, 1), 0)
    outs = []
    for g, w in enumerate(POOL_WINDOWS):
        cols = slice(g * gdim, (g + 1) * gdim)
        e = uext_ref[:, cols]
        acc = e
        span = 1
        while span < w:
            acc = acc + pltpu.roll(acc, span, axis=0)
            span *= 2
        inv = 1.0 / jnp.minimum(pos + 1, w).astype(F32)
        p = acc[POOL_HIST:n] * inv - e[POOL_HIST:n]
        o = jnp.dot(p.astype(BF16), wpool_ref[g], preferred_element_type=F32)
        outs.append(o)
    return jnp.concatenate(outs, axis=-1) * pscale_ref[...]


def _merge(x, att_g, y_b, gate_a, gate_b, wa_ref, wout_ref, fg):
    y_a = jnp.dot(att_g, wa_ref[...], preferred_element_type=F32)
    m = gate_a * y_a + gate_b * y_b
    out = x + jnp.dot(m.astype(BF16), wout_ref[...], preferred_element_type=F32)
    return _rms(out, fg)


def _prompt_kernel(x_ref, ng_ref, win_ref, wpool_ref, pscale_ref, wa_ref, wb_ref, wout_ref, fg_ref,
                   y_ref, k_ref, v_ref, pool_ref,
                   kh_ref, vh_ref, q_ref, sga_ref, uext_ref, sgb_ref, gate_ref, attg_ref,
                   acc_ref, used_ref, tri_ref, yb_ref):
    t = pl.program_id(1)
    nt = pl.num_programs(1)
    d = x_ref.shape[2]
    row0 = pl.multiple_of(t * TQ, TQ)

    @pl.when(t == 0)
    def _():
        tri_ref[...] = _tri(TQ)
        uext_ref[0:POOL_HIST, :] = jnp.zeros((POOL_HIST, d), F32)

    @pl.when(t > 0)
    def _():
        uext_ref[0:POOL_HIST, :] = uext_ref[TQ:TQ + POOL_HIST, :]

    x = x_ref[0]
    h = _rms(x, ng_ref[...]).astype(BF16)

    def proj(c):
        return jnp.dot(h, win_ref[:, c * d:(c + 1) * d], preferred_element_type=F32)

    def head(z, hd):
        return z[:, hd * HEAD_DIM:(hd + 1) * HEAD_DIM]

    q = (proj(0) * SCALE).astype(BF16)
    for hd in range(HEADS):
        q_ref[hd] = head(q, hd)
    k = proj(1)
    kb = k.astype(BF16)
    for hd in range(HEADS):
        k_ref[0, pl.ds(hd, TQ, stride=HEADS), :] = head(k, hd)
        kh_ref[hd, pl.ds(row0, TQ), :] = head(kb, hd)
    v = proj(2)
    vb = v.astype(BF16)
    for hd in range(HEADS):
        v_ref[0, pl.ds(hd, TQ, stride=HEADS), :] = head(v, hd)
        vh_ref[hd, pl.ds(row0, TQ), :] = head(vb, hd)
    def proj_ga():
        sga_ref[...] = _silu(proj(3)).astype(BF16)

    def proj_u():
        uext_ref[POOL_HIST:POOL_HIST + TQ, :] = proj(4)

    def proj_gb():
        sgb_ref[...] = _silu(proj(5)).astype(BF16)

    def proj_gate_a():
        gate_ref[:, 0:d] = _sigmoid(proj(6))

    def proj_gate_b():
        gate_ref[:, d:2 * d] = _sigmoid(proj(7))

    def branch_b():
        pb = _pool(uext_ref, TQ, t * TQ, wpool_ref, pscale_ref)
        pb_g = (pb * sgb_ref[...].astype(F32)).astype(BF16)
        yb_ref[...] = jnp.dot(pb_g, wb_ref[...], preferred_element_type=F32)

    heads = range(HEADS)
    half = TQ // 2
    top, bottom = slice(0, half), slice(half, TQ)
    fillers = iter((proj_ga, proj_u, proj_gb, proj_gate_a, None, None, proj_gate_b, branch_b))

    _sb_block([head(q, hd)[bottom] for hd in heads], [head(kb, hd) for hd in heads],
              [head(vb, hd) for hd in heads], tri_ref[...], acc_ref.at[:, bottom],
              used_ref.at[:, bottom], half, 0.0, fillers)
    lead_top = jnp.where(t > 0, half, 0)
    r_prev = pl.multiple_of(jnp.maximum(row0 - half, 0), half)

    def top_window(xh_ref, xb):
        return [jnp.concatenate([xh_ref[hd, pl.ds(r_prev, half), :], head(xb, hd)[top]], axis=0)
                for hd in heads]

    _sb_block([head(q, hd)[top] for hd in heads], top_window(kh_ref, kb), top_window(vh_ref, vb),
              tri_ref[...], acc_ref.at[:, top], used_ref.at[:, top], lead_top, 0.0, fillers)
    used_top, used_bottom = _least_used(used_ref.at[:, top]), _least_used(used_ref.at[:, bottom])
    for thunk in fillers:
        if thunk is not None:
            thunk()

    def earlier_keys_while_needed(rows, least_used0, n_chunks, chunk_start, nk):
        def more(c):
            i, least_used = c
            return jnp.logical_and(i < n_chunks, least_used < USED_CAP)

        def body(c):
            i, _ = c
            r = pl.multiple_of(chunk_start(i), nk)
            _sb_block([q_ref[hd, rows, :] for hd in heads],
                      [kh_ref[hd, pl.ds(r, nk), :] for hd in heads],
                      [vh_ref[hd, pl.ds(r, nk), :] for hd in heads], tri_ref[0:nk, 0:nk],
                      acc_ref.at[:, rows], used_ref.at[:, rows], None, None)
            return i + 1, _least_used(used_ref.at[:, rows])

        lax.while_loop(more, body, (jnp.int32(0), least_used0))

    earlier_keys_while_needed(bottom, used_bottom, t, lambda i: (t - 1 - i) * TQ, TQ)
    earlier_keys_while_needed(top, used_top, 2 * t - 1, lambda i: row0 - half * (i + 2), half)
    for hd in range(HEADS):
        cols = slice(hd * HEAD_DIM, (hd + 1) * HEAD_DIM)
        attg_ref[:, cols] = (acc_ref[hd] * sga_ref[:, cols].astype(F32)).astype(BF16)

    y_ref[0] = _merge(x_ref[0], attg_ref[...], yb_ref[...], gate_ref[:, 0:d], gate_ref[:, d:2 * d],
                      wa_ref, wout_ref, fg_ref[...])

    @pl.when(t == nt - 1)
    def _():
        pool_ref[0] = uext_ref[TQ:TQ + POOL_HIST, :]


def _const_spec(shape):
    nd = len(shape)
    return pl.BlockSpec(shape, lambda *_: (0,) * nd, pipeline_mode=pl.Buffered(1))


def _prompt_layer(x, ng, win, wpool, pscale, wa, wb, wout, fg):
    b, t, d = x.shape
    nt = t // TQ
    tile = pl.BlockSpec((1, TQ, d), lambda i, j: (i, j, 0))
    kv_tile = pl.BlockSpec((1, TQ * HEADS, HEAD_DIM), lambda i, j: (i, j, 0))
    kv_shape = jax.ShapeDtypeStruct((b, t * HEADS, HEAD_DIM), F32)
    out_shape = (
        jax.ShapeDtypeStruct((b, t, d), F32),
        kv_shape,
        kv_shape,
        jax.ShapeDtypeStruct((b, POOL_HIST, d), F32),
    )
    return pl.pallas_call(
        _prompt_kernel,
        grid=(b, nt),
        in_specs=[tile, _const_spec(ng.shape), _const_spec(win.shape), _const_spec(wpool.shape),
                  _const_spec(pscale.shape), _const_spec(wa.shape), _const_spec(wb.shape),
                  _const_spec(wout.shape), _const_spec(fg.shape)],
        out_specs=(tile, kv_tile, kv_tile, pl.BlockSpec((1, POOL_HIST, d), lambda i, j: (i, 0, 0))),
        out_shape=out_shape,
        scratch_shapes=[
            pltpu.VMEM((HEADS, t, HEAD_DIM), BF16),
            pltpu.VMEM((HEADS, t, HEAD_DIM), BF16),
            pltpu.VMEM((HEADS, TQ, HEAD_DIM), BF16),
            pltpu.VMEM((TQ, d), BF16),
            pltpu.VMEM((POOL_HIST + TQ, d), F32),
            pltpu.VMEM((TQ, d), BF16),
            pltpu.VMEM((TQ, 2 * d), F32),
            pltpu.VMEM((TQ, d), BF16),
            pltpu.VMEM((HEADS, TQ, HEAD_DIM), F32),
            pltpu.VMEM((HEADS, TQ, HEAD_DIM), F32),
            pltpu.VMEM((TQ, TQ), BF16),
            pltpu.VMEM((TQ, d), F32),
        ],
        compiler_params=pltpu.CompilerParams(
            dimension_semantics=("arbitrary", "arbitrary"), vmem_limit_bytes=VMEM_LIMIT),
        name="prompt_layer",
    )(x, ng, win, wpool, pscale, wa, wb, wout, fg)


def _sample_proj_kernel(x_ref, ng_ref, win_ref, z_ref):
    h = _rms(x_ref[...], ng_ref[...]).astype(BF16)
    z_ref[0] = jnp.dot(h, win_ref[...], preferred_element_type=F32)


def _sample_proj(x2, ng, win):
    n, d = x2.shape
    nc = win.shape[1] // d
    return pl.pallas_call(
        _sample_proj_kernel,
        grid=(nc,),
        in_specs=[pl.BlockSpec((n, d), lambda c: (0, 0)), pl.BlockSpec(ng.shape, lambda c: (0, 0)),
                  pl.BlockSpec((d, d), lambda c: (0, c))],
        out_specs=pl.BlockSpec((1, n, d), lambda c: (c, 0, 0)),
        out_shape=jax.ShapeDtypeStruct((nc, n, d), F32),
        compiler_params=pltpu.CompilerParams(dimension_semantics=("arbitrary",)),
        name="sample_proj",
    )(x2, ng, win)


def _sample_kernel(x_ref, z_ref, hist_ref, wpool_ref, pscale_ref, wa_ref, wb_ref, wout_ref, fg_ref,
                   ck_hbm, cv_hbm, y_ref, pool_ref,
                   acc_ref, used_ref, uext_ref, tri_ref, kbuf, vbuf, sem, *, past_len):
    b = pl.program_id(0)
    ts = x_ref.shape[1]
    nkb = past_len // TKC
    blk_rows = TKC * HEADS

    def cache_copies(c, slot):
        rows = pl.ds(pl.multiple_of(c * blk_rows, blk_rows), blk_rows)
        return (pltpu.make_async_copy(ck_hbm.at[b, rows], kbuf.at[slot], sem.at[0, slot]),
                pltpu.make_async_copy(cv_hbm.at[b, rows], vbuf.at[slot], sem.at[1, slot]))

    def start(c, slot):
        for cp in cache_copies(c, slot):
            cp.start()

    def wait(c, slot):
        for cp in cache_copies(c, slot):
            cp.wait()

    start(nkb - 1, 0)

    heads = range(HEADS)

    def chunk_heads(c, scale=None):
        z = z_ref[c] if scale is None else z_ref[c] * scale
        z = z.astype(BF16)
        return [z[:, hd * HEAD_DIM:(hd + 1) * HEAD_DIM] for hd in heads]

    qs = chunk_heads(0, SCALE)
    _sb_block(qs, chunk_heads(1), chunk_heads(2), _tri(ts), acc_ref, used_ref, 0, 0.0)

    uext_ref[0:POOL_HIST, :] = hist_ref[0]
    uext_ref[POOL_HIST:POOL_HIST + ts, :] = z_ref[4]
    pool_ref[0] = uext_ref[ts:ts + POOL_HIST, :]
    pb = _pool(uext_ref, ts, past_len, wpool_ref, pscale_ref)
    pb_g = (pb * _silu(z_ref[5])).astype(BF16)
    y_b = jnp.dot(pb_g, wb_ref[...], preferred_element_type=F32)

    tri_ref[...] = _tri(TKC)

    def more(c):
        i, least_used = c
        return jnp.logical_and(i < nkb, least_used < USED_CAP)

    def body(c):
        i, _ = c
        slot = lax.rem(i, 2)
        wait(nkb - 1 - i, slot)

        @pl.when(i + 1 < nkb)
        def _():
            start(nkb - 2 - i, 1 - slot)

        _sb_block(chunk_heads(0, SCALE),
                  [_head_rows(kbuf, slot, 0, TKC, hd).astype(BF16) for hd in heads],
                  [_head_rows(vbuf, slot, 0, TKC, hd).astype(BF16) for hd in heads],
                  tri_ref[...], acc_ref, used_ref, None, None)
        return i + 1, _least_used(used_ref)

    done, _ = lax.while_loop(more, body, (jnp.int32(0), _least_used(used_ref)))

    @pl.when(done < nkb)
    def _():
        wait(nkb - 1 - done, lax.rem(done, 2))

    att = jnp.concatenate([acc_ref[hd] for hd in range(HEADS)], axis=-1)
    att_g = (att * _silu(z_ref[3])).astype(BF16)
    y_ref[0] = _merge(x_ref[0], att_g, y_b, _sigmoid(z_ref[6]), _sigmoid(z_ref[7]),
                      wa_ref, wout_ref, fg_ref[...])


def _sample_layer(x, z, ck, cv, hist, wpool, pscale, wa, wb, wout, fg):
    b, ts, d = x.shape
    past_len = ck.shape[1] // HEADS
    nc = z.shape[0]
    hbm = pl.BlockSpec(memory_space=pl.ANY)
    return pl.pallas_call(
        functools.partial(_sample_kernel, past_len=past_len),
        grid=(b,),
        in_specs=[pl.BlockSpec((1, ts, d), lambda i: (i, 0, 0)),
                  pl.BlockSpec((nc, ts, d), lambda i: (0, i, 0)),
                  pl.BlockSpec((1, POOL_HIST, d), lambda i: (i, 0, 0)),
                  _const_spec(wpool.shape), _const_spec(pscale.shape), _const_spec(wa.shape),
                  _const_spec(wb.shape), _const_spec(wout.shape), _const_spec(fg.shape),
                  hbm, hbm],
        out_specs=(pl.BlockSpec((1, ts, d), lambda i: (i, 0, 0)),
                   pl.BlockSpec((1, POOL_HIST, d), lambda i: (i, 0, 0))),
        out_shape=(jax.ShapeDtypeStruct((b, ts, d), F32),
                   jax.ShapeDtypeStruct((b, POOL_HIST, d), F32)),
        scratch_shapes=[
            pltpu.VMEM((HEADS, ts, HEAD_DIM), F32),
            pltpu.VMEM((HEADS, ts, HEAD_DIM), F32),
            pltpu.VMEM((POOL_HIST + ts, d), F32),
            pltpu.VMEM((TKC, TKC), BF16),
            pltpu.VMEM((2, TKC * HEADS, HEAD_DIM), F32),
            pltpu.VMEM((2, TKC * HEADS, HEAD_DIM), F32),
            pltpu.SemaphoreType.DMA((2, 2)),
        ],
        compiler_params=pltpu.CompilerParams(
            dimension_semantics=("arbitrary",), vmem_limit_bytes=VMEM_LIMIT),
        name="sample_layer",
    )(x, z, hist, wpool, pscale, wa, wb, wout, fg, ck, cv)


def kernel(x_prompt, x_sample, cache_k, cache_v, state_pool, norm_g, w_in, w_pool, pool_scale,
           w_br_a, w_br_b, w_out, final_g):
    depth = norm_g.shape[0]
    assert depth == 1, "single-layer stack"
    b, t, d = x_prompt.shape
    bs, ts, _ = x_sample.shape
    past_len = cache_k.shape[2]
    assert t % TQ == 0 and past_len % TKC == 0
    assert cache_k.shape[3:] == (HEADS, HEAD_DIM) and d == HEADS * HEAD_DIM

    ng = norm_g[0].reshape(1, d)
    fg = final_g.reshape(1, d)
    pscale = pool_scale[0].reshape(1, d)
    win = w_in[0].astype(BF16)
    wpool = w_pool[0].astype(BF16)
    wa = w_br_a[0].astype(BF16)
    wb = w_br_b[0].astype(BF16)
    wout = w_out[0].astype(BF16)

    y_p, k_p, v_p, pool_p = _prompt_layer(x_prompt, ng, win, wpool, pscale, wa, wb, wout, fg)

    z = _sample_proj(x_sample.reshape(bs * ts, d), ng, win)
    hist = jnp.pad(state_pool[0], ((0, 0), (POOL_HIST - state_pool.shape[2], 0), (0, 0)))
    y_s, pool_s = _sample_layer(x_sample, z, cache_k.reshape(bs, past_len * HEADS, HEAD_DIM),
                                cache_v.reshape(bs, past_len * HEADS, HEAD_DIM), hist, wpool, pscale,
                                wa, wb, wout, fg)

    nbuf = state_pool.shape[2]
    hs = (HEADS, HEAD_DIM)
    return (y_p, y_s,
            k_p.reshape(1, b, t, *hs), v_p.reshape(1, b, t, *hs), pool_p[None, :, POOL_HIST - nbuf:],
            z[1].reshape(1, bs, ts, *hs), z[2].reshape(1, bs, ts, *hs), pool_s[None, :, POOL_HIST - nbuf:])
```
